```python
import jax
import jax.numpy as jnp
from jax import lax

D_MODEL = 1024
BATCH = 8
SEQ = 4096
DEPTH = 2

D_RNN = D_MODEL
D_POOL = D_MODEL
D_MIX = D_RNN + D_POOL
N_RNN_HEADS = 8
RNN_HEAD_DIM = D_RNN // N_RNN_HEADS
CONV_WIDTH = 4
LRU_C = 8.0
POOL_WINDOWS = (2, 4, 8, 16)
N_POOL_GROUPS = len(POOL_WINDOWS)
POOL_GROUP_DIM = D_POOL // N_POOL_GROUPS
NORM_EPS = 1e-6

kernel_name = "hybrid_rglru_multiscale_pool_parallel_heads"


def rmsnorm(x, g):
    xf = x.astype(jnp.float32)
    y = xf * lax.rsqrt(jnp.mean(xf * xf, axis=-1, keepdims=True) + NORM_EPS)
    return (y * g.astype(jnp.float32)).astype(x.dtype)


def causal_depthwise_conv(x, w, b):
    y = lax.conv_general_dilated(
        x, w[:, None, :].astype(x.dtype), window_strides=(1,),
        padding=[(CONV_WIDTH - 1, 0)],
        dimension_numbers=("NWC", "WIO", "NWC"),
        feature_group_count=x.shape[-1])
    return y + b


def rg_lru(x, w_a, b_a, w_x, b_x, lam):
    B, S, _ = x.shape
    xh = x.reshape(B, S, N_RNN_HEADS, RNN_HEAD_DIM)
    r = jax.nn.sigmoid(jnp.einsum("bshi,hij->bshj", xh, w_a) + b_a).reshape(B, S, D_RNN)
    i = jax.nn.sigmoid(jnp.einsum("bshi,hij->bshj", xh, w_x) + b_x).reshape(B, S, D_RNN)
    log_a = -LRU_C * r.astype(jnp.float32) * jax.nn.softplus(-lam.astype(jnp.float32))
    a = jnp.exp(log_a)
    mult = jnp.sqrt(-jnp.expm1(2.0 * log_a))
    u = mult * (i * x).astype(jnp.float32)

    def step(h, inp):
        a_t, u_t = inp
        h = a_t * h + u_t
        return h, h

    h0 = jnp.zeros((B, D_RNN), jnp.float32)
    _, hs = lax.scan(step, h0, (jnp.swapaxes(a, 0, 1), jnp.swapaxes(u, 0, 1)))
    return jnp.swapaxes(hs, 0, 1).astype(x.dtype)


def multi_scale_pool(x, w, b, scale):
    B, S, _ = x.shape
    xg = x.reshape(B, S, N_POOL_GROUPS, POOL_GROUP_DIM).astype(jnp.float32)
    cs = jnp.cumsum(xg, axis=1)
    t = jnp.arange(S)
    means = []
    for g, win in enumerate(POOL_WINDOWS):
        csg = cs[:, :, g]
        lagged = jnp.pad(csg, ((0, 0), (win, 0), (0, 0)))[:, :S]
        count = jnp.minimum(t + 1, win).astype(jnp.float32)[None, :, None]
        means.append((csg - lagged) / count)
    pooled = (jnp.stack(means, axis=2) - xg).astype(x.dtype)
    y = jnp.einsum("bsgi,gij->bsgj", pooled, w) + b
    return y.reshape(B, S, D_POOL) * scale


def setup_inputs(seed: int = 0) -> dict:
    key = jax.random.key(seed)
    ks = jax.random.split(key, 24)
    f32 = jnp.float32
    nrm = lambda k, shape, s: jax.random.normal(k, shape, f32) * s
    L = DEPTH
    x = jax.random.normal(ks[0], (BATCH, SEQ, D_MODEL), f32)
    c = jax.random.normal(ks[1], (BATCH, D_MODEL), f32)
    ada_w = nrm(ks[2], (L, D_MODEL, 3 * D_MODEL), 0.5 * D_MODEL ** -0.5)
    ada_b = nrm(ks[3], (L, 3 * D_MODEL), 0.01)
    pre_norm_g = 1.0 + nrm(ks[4], (L, D_MODEL), 0.05)
    w_in = nrm(ks[5], (L, D_MODEL, 2 * D_MIX), D_MODEL ** -0.5)
    conv_w = nrm(ks[6], (L, CONV_WIDTH, D_RNN), CONV_WIDTH ** -0.5)
    conv_b = nrm(ks[7], (L, D_RNN), 0.01)
    gate_a_w = nrm(ks[8], (L, N_RNN_HEADS, RNN_HEAD_DIM, RNN_HEAD_DIM), RNN_HEAD_DIM ** -0.5)
    gate_a_b = nrm(ks[9], (L, N_RNN_HEADS, RNN_HEAD_DIM), 0.01)
    gate_x_w = nrm(ks[10], (L, N_RNN_HEADS, RNN_HEAD_DIM, RNN_HEAD_DIM), RNN_HEAD_DIM ** -0.5)
    gate_x_b = nrm(ks[11], (L, N_RNN_HEADS, RNN_HEAD_DIM), 0.01)
    a_c = jax.random.uniform(ks[12], (L, D_RNN), f32, 0.9, 0.999)
    a0 = a_c ** (1.0 / LRU_C)
    lru_lambda = jnp.log(a0) - jnp.log1p(-a0)
    pool_w = nrm(ks[13], (L, N_POOL_GROUPS, POOL_GROUP_DIM, POOL_GROUP_DIM), POOL_GROUP_DIM ** -0.5)
    pool_b = nrm(ks[14], (L, N_POOL_GROUPS, POOL_GROUP_DIM), 0.01)
    pool_scale = jax.random.uniform(ks[15], (L, D_POOL), f32, 0.5, 1.5)
    w_out = nrm(ks[16], (L, D_MIX, D_MODEL), D_MIX ** -0.5)
    post_norm_g = 1.0 + nrm(ks[17], (L, D_MODEL), 0.05)
    return {"x": x, "c": c, "ada_w": ada_w, "ada_b": ada_b, "pre_norm_g": pre_norm_g,
            "w_in": w_in, "conv_w": conv_w, "conv_b": conv_b,
            "gate_a_w": gate_a_w, "gate_a_b": gate_a_b, "gate_x_w": gate_x_w, "gate_x_b": gate_x_b,
            "lru_lambda": lru_lambda, "pool_w": pool_w, "pool_b": pool_b, "pool_scale": pool_scale,
            "w_out": w_out, "post_norm_g": post_norm_g}


def reference(x, c, ada_w, ada_b, pre_norm_g, w_in, conv_w, conv_b,
              gate_a_w, gate_a_b, gate_x_w, gate_x_b, lru_lambda,
              pool_w, pool_b, pool_scale, w_out, post_norm_g):
    c_act = jax.nn.silu(c)
    for l in range(DEPTH):
        mod = c_act @ ada_w[l] + ada_b[l]
        shift, scale, gate = jnp.split(mod, 3, axis=-1)
        h = rmsnorm(x, pre_norm_g[l]) * (1.0 + scale[:, None, :]) + shift[:, None, :]
        proj = h @ w_in[l]
        x_rnn, g_rnn, x_pool, g_pool = jnp.split(
            proj, [D_RNN, 2 * D_RNN, 2 * D_RNN + D_POOL], axis=-1)
        u = causal_depthwise_conv(x_rnn, conv_w[l], conv_b[l])
        y_rnn = rg_lru(u, gate_a_w[l], gate_a_b[l], gate_x_w[l], gate_x_b[l],
                       lru_lambda[l]) * jax.nn.silu(g_rnn)
        y_pool = multi_scale_pool(x_pool, pool_w[l], pool_b[l], pool_scale[l]) * jax.nn.silu(g_pool)
        y = jnp.concatenate([y_rnn, y_pool], axis=-1) @ w_out[l]
        x = x + gate[:, None, :] * rmsnorm(y, post_norm_g[l])
    return x
```

```python
import functools

import jax
import jax.numpy as jnp
from jax import lax
from jax.experimental import pallas as pl
from jax.experimental.pallas import tpu as pltpu

D_MODEL = 1024
BATCH = 8
SEQ = 4096
DEPTH = 2
D_RNN = D_MODEL
D_POOL = D_MODEL
N_RNN_HEADS = 8
RNN_HEAD_DIM = D_RNN // N_RNN_HEADS
CONV_WIDTH = 4
LRU_C = 8.0
POOL_WINDOWS = (2, 4, 8, 16)
POOL_GROUP_DIM = D_POOL // len(POOL_WINDOWS)
NORM_EPS = 1e-6

T_CHUNK = 64
ROWS = T_CHUNK * BATCH
CONV_HIST = CONV_WIDTH - 1
POOL_HIST = max(POOL_WINDOWS)
T_SUB = 8
VMEM_LIMIT_BYTES = 56 * 1024 * 1024

F32 = jnp.float32
BF16 = jnp.bfloat16


def _sigmoid(v):
    return 0.5 * jnp.tanh(0.5 * v) + 0.5


def _mod_kernel(c_ref, w_ref, b_ref, o_ref):
    c = c_ref[...]
    c_act = c * _sigmoid(c)
    o_ref[0, 0] = jnp.dot(c_act, w_ref[0], preferred_element_type=F32) + b_ref[0, 0]


def _modulation(c, ada_w, ada_b):
    depth = ada_w.shape[0]
    return pl.pallas_call(
        _mod_kernel,
        grid=(depth, 3),
        in_specs=[
            pl.BlockSpec((BATCH, D_MODEL), lambda l, j: (0, 0)),
            pl.BlockSpec((1, D_MODEL, D_MODEL), lambda l, j: (l, 0, j)),
            pl.BlockSpec((1, 1, 1, D_MODEL), lambda l, j: (l, j, 0, 0)),
        ],
        out_specs=pl.BlockSpec((1, 1, BATCH, D_MODEL), lambda l, j: (l, j, 0, 0)),
        out_shape=jax.ShapeDtypeStruct((depth, 3, BATCH, D_MODEL), F32),
        name="ada_modulation",
    )(c, ada_w, ada_b.reshape(depth, 3, 1, D_MODEL))


def _layer_kernel(x_ref, mod_ref, pre_g_ref, post_g_ref, w_in_ref, conv_w_ref, conv_b_ref,
                  wg_ref, bg_ref, lam_ref, pool_w_ref, pool_b_ref, pool_s_ref, w_out_ref,
                  o_ref,
                  hb_scr, xr_scr, gr_scr, xp_scr, gp_scr, a_scr, u_scr, ycat_scr, hst_scr):
    step = pl.program_id(0)
    n_sub = T_CHUNK // T_SUB

    @pl.when(step == 0)
    def _():
        xr_scr[0:CONV_HIST] = jnp.zeros((CONV_HIST, BATCH, D_RNN), F32)
        xp_scr[0:POOL_HIST] = jnp.zeros((POOL_HIST, BATCH, D_POOL), F32)
        hst_scr[...] = jnp.zeros((BATCH, D_RNN), F32)

    @pl.when(step > 0)
    def _():
        xr_scr[0:CONV_HIST] = xr_scr[T_CHUNK:T_CHUNK + CONV_HIST]
        xp_scr[0:POOL_HIST] = xp_scr[T_CHUNK:T_CHUNK + POOL_HIST]

    shift = mod_ref[0]
    gmod = pre_g_ref[...] * (1.0 + mod_ref[1])

    def pre_body(j, carry):
        t0 = pl.multiple_of(j * T_SUB, T_SUB)
        xs = x_ref[pl.ds(t0, T_SUB)]
        ms = jnp.mean(xs * xs, axis=-1, keepdims=True)
        hh = xs * lax.rsqrt(ms + NORM_EPS) * gmod + shift
        r0 = pl.multiple_of(t0 * BATCH, T_SUB * BATCH)
        hb_scr[pl.ds(r0, T_SUB * BATCH), :] = hh.reshape(T_SUB * BATCH, D_MODEL).astype(BF16)
        return carry

    lax.fori_loop(0, n_sub, pre_body, 0)

    hb = hb_scr[...]
    xr_scr[CONV_HIST:CONV_HIST + T_CHUNK] = jnp.dot(
        hb, w_in_ref[:, 0:D_RNN], preferred_element_type=F32).reshape(T_CHUNK, BATCH, D_RNN)
    gr_scr[...] = jnp.dot(
        hb, w_in_ref[:, D_RNN:2 * D_RNN], preferred_element_type=F32).reshape(T_CHUNK, BATCH, D_RNN)
    xp_scr[POOL_HIST:POOL_HIST + T_CHUNK] = jnp.dot(
        hb, w_in_ref[:, 2 * D_RNN:2 * D_RNN + D_POOL],
        preferred_element_type=F32).reshape(T_CHUNK, BATCH, D_POOL)
    gp_scr[...] = jnp.dot(
        hb, w_in_ref[:, 2 * D_RNN + D_POOL:], preferred_element_type=F32).reshape(T_CHUNK, BATCH, D_POOL)

    conv_b = conv_b_ref[...]

    def conv_body(j, carry):
        t0 = pl.multiple_of(j * T_SUB, T_SUB)
        acc = conv_b + conv_w_ref[0:1, :] * xr_scr[pl.ds(t0, T_SUB)]
        for k in range(1, CONV_WIDTH):
            acc = acc + conv_w_ref[k:k + 1, :] * xr_scr[pl.ds(t0 + k, T_SUB)]
        u_scr[pl.ds(t0, T_SUB)] = acc
        r0 = pl.multiple_of(t0 * BATCH, T_SUB * BATCH)
        hb_scr[pl.ds(r0, T_SUB * BATCH), :] = acc.reshape(T_SUB * BATCH, D_RNN).astype(BF16)
        return carry

    lax.fori_loop(0, n_sub, conv_body, 0)

    lam = lam_ref[...]
    softplus_neg_lam = jnp.maximum(-lam, 0.0) + jnp.log(1.0 + jnp.exp(-jnp.abs(lam)))
    log_a_scale = -LRU_C * softplus_neg_lam
    for h in range(N_RNN_HEADS):
        sl = slice(h * RNN_HEAD_DIM, (h + 1) * RNN_HEAD_DIM)
        ga = jnp.dot(hb_scr[:, sl], wg_ref[h], preferred_element_type=F32) + bg_ref[h]
        ga = ga.reshape(T_CHUNK, BATCH, 2 * RNN_HEAD_DIM)
        r_gate = _sigmoid(ga[:, :, :RNN_HEAD_DIM])
        i_gate = _sigmoid(ga[:, :, RNN_HEAD_DIM:])
        log_a = r_gate * log_a_scale[:, sl]
        a = jnp.exp(log_a)
        a_scr[:, :, sl] = a
        mult = jnp.sqrt(-jnp.tanh(log_a) * (a * a + 1.0))
        u_scr[:, :, sl] = mult * (i_gate * u_scr[:, :, sl])

    def scan_body(t, h):
        h = a_scr[t] * h + u_scr[t]
        u_scr[t] = h
        return h

    hst_scr[...] = lax.fori_loop(0, T_CHUNK, scan_body, hst_scr[...], unroll=8)

    def rnn_out_body(j, carry):
        t0 = pl.multiple_of(j * T_SUB, T_SUB)
        g = gr_scr[pl.ds(t0, T_SUB)]
        y = u_scr[pl.ds(t0, T_SUB)] * (g * _sigmoid(g))
        r0 = pl.multiple_of(t0 * BATCH, T_SUB * BATCH)
        ycat_scr[pl.ds(r0, T_SUB * BATCH), 0:D_RNN] = y.reshape(T_SUB * BATCH, D_RNN).astype(BF16)
        return carry

    lax.fori_loop(0, n_sub, rnn_out_body, 0)

    t_base = step * T_CHUNK
    for g, win in enumerate(POOL_WINDOWS):
        sl = slice(g * POOL_GROUP_DIM, (g + 1) * POOL_GROUP_DIM)

        def pool_body(j, carry, sl=sl, win=win):
            t0 = pl.multiple_of(j * T_SUB, T_SUB)
            xv = xp_scr[pl.ds(t0 + POOL_HIST - (win - 1), T_SUB + win - 1), :, sl]
            s = xv
            d = 1
            while d < win:
                s = s[d:] + s[:-d]
                d *= 2
            t_glob = t_base + t0 + lax.broadcasted_iota(jnp.int32, (T_SUB, 1, 1), 0)
            inv_count = 1.0 / jnp.minimum(t_glob + 1, win).astype(F32)
            pooled = s * inv_count - xv[win - 1:]
            r0 = pl.multiple_of(t0 * BATCH, T_SUB * BATCH)
            hb_scr[pl.ds(r0, T_SUB * BATCH), sl] = pooled.reshape(
                T_SUB * BATCH, POOL_GROUP_DIM).astype(BF16)
            return carry

        lax.fori_loop(0, n_sub, pool_body, 0)

        yp = jnp.dot(hb_scr[:, sl], pool_w_ref[g], preferred_element_type=F32) + pool_b_ref[g]
        yp = yp.reshape(T_CHUNK, BATCH, POOL_GROUP_DIM) * pool_s_ref[:, sl]
        gp = gp_scr[:, :, sl]
        yp = yp * (gp * _sigmoid(gp))
        ycat_scr[:, D_RNN + g * POOL_GROUP_DIM:D_RNN + (g + 1) * POOL_GROUP_DIM] = yp.reshape(
            ROWS, POOL_GROUP_DIM).astype(BF16)

    a_scr[...] = jnp.dot(ycat_scr[...], w_out_ref[...],
                         preferred_element_type=F32).reshape(T_CHUNK, BATCH, D_MODEL)
    gate_g = mod_ref[2] * post_g_ref[...]

    def post_body(j, carry):
        t0 = pl.multiple_of(j * T_SUB, T_SUB)
        y = a_scr[pl.ds(t0, T_SUB)]
        ms = jnp.mean(y * y, axis=-1, keepdims=True)
        o_ref[pl.ds(t0, T_SUB)] = x_ref[pl.ds(t0, T_SUB)] + y * lax.rsqrt(ms + NORM_EPS) * gate_g
        return carry

    lax.fori_loop(0, n_sub, post_body, 0)


def _const_spec(shape):
    zeros = (0,) * len(shape)
    return pl.BlockSpec(shape, lambda i: zeros, pipeline_mode=pl.Buffered(1))


def _layer(xt, mod, pre_g, post_g, w_in, conv_w, conv_b, wg, bg, lam, pool_w, pool_b, pool_s, w_out):
    seq = xt.shape[0]
    d_mix = D_RNN + D_POOL
    act_spec = pl.BlockSpec((T_CHUNK, BATCH, D_MODEL), lambda i: (i, 0, 0))
    return pl.pallas_call(
        _layer_kernel,
        grid=(seq // T_CHUNK,),
        in_specs=[
            act_spec,
            _const_spec((3, BATCH, D_MODEL)),
            _const_spec((1, D_MODEL)),
            _const_spec((1, D_MODEL)),
            _const_spec((D_MODEL, 2 * d_mix)),
            _const_spec((CONV_WIDTH, D_RNN)),
            _const_spec((1, D_RNN)),
            _const_spec((N_RNN_HEADS, RNN_HEAD_DIM, 2 * RNN_HEAD_DIM)),
            _const_spec((N_RNN_HEADS, 1, 2 * RNN_HEAD_DIM)),
            _const_spec((1, D_RNN)),
            _const_spec((len(POOL_WINDOWS), POOL_GROUP_DIM, POOL_GROUP_DIM)),
            _const_spec((len(POOL_WINDOWS), 1, POOL_GROUP_DIM)),
            _const_spec((1, D_POOL)),
            _const_spec((d_mix, D_MODEL)),
        ],
        out_specs=act_spec,
        out_shape=jax.ShapeDtypeStruct(xt.shape, F32),
        scratch_shapes=[
            pltpu.VMEM((ROWS, D_MODEL), BF16),
            pltpu.VMEM((CONV_HIST + T_CHUNK, BATCH, D_RNN), F32),
            pltpu.VMEM((T_CHUNK, BATCH, D_RNN), F32),
            pltpu.VMEM((POOL_HIST + T_CHUNK, BATCH, D_POOL), F32),
            pltpu.VMEM((T_CHUNK, BATCH, D_POOL), F32),
            pltpu.VMEM((T_CHUNK, BATCH, D_RNN), F32),
            pltpu.VMEM((T_CHUNK, BATCH, D_RNN), F32),
            pltpu.VMEM((ROWS, d_mix), BF16),
            pltpu.VMEM((BATCH, D_RNN), F32),
        ],
        compiler_params=pltpu.CompilerParams(
            dimension_semantics=("arbitrary",),
            vmem_limit_bytes=VMEM_LIMIT_BYTES,
        ),
        name="hybrid_layer",
    )(xt, mod, pre_g, post_g, w_in, conv_w, conv_b, wg, bg, lam, pool_w, pool_b, pool_s, w_out)


def kernel(x, c, ada_w, ada_b, pre_norm_g, w_in, conv_w, conv_b, gate_a_w, gate_a_b, gate_x_w,
           gate_x_b, lru_lambda, pool_w, pool_b, pool_scale, w_out, post_norm_g):
    depth = ada_w.shape[0]
    mod = _modulation(c, ada_w, ada_b)
    xt = jnp.transpose(x, (1, 0, 2))
    for l in range(depth):
        wg = jnp.concatenate([gate_a_w[l], gate_x_w[l]], axis=-1).astype(BF16)
        bg = jnp.concatenate([gate_a_b[l], gate_x_b[l]], axis=-1)[:, None, :]
        xt = _layer(
            xt, mod[l], pre_norm_g[l][None, :], post_norm_g[l][None, :],
            w_in[l].astype(BF16), conv_w[l], conv_b[l][None, :], wg, bg,
            lru_lambda[l][None, :], pool_w[l].astype(BF16), pool_b[l][:, None, :],
            pool_scale[l][None, :], w_out[l].astype(BF16))
    return jnp.transpose(xt, (1, 0, 2))
```

```python
import functools

import jax
import jax.numpy as jnp
from jax import lax
from jax.experimental import pallas as pl
from jax.experimental.pallas import tpu as pltpu

D_MODEL = 1024
BATCH = 8
SEQ = 4096
DEPTH = 2
D_RNN = D_MODEL
D_POOL = D_MODEL
N_RNN_HEADS = 8
RNN_HEAD_DIM = D_RNN // N_RNN_HEADS
CONV_WIDTH = 4
LRU_C = 8.0
POOL_WINDOWS = (2, 4, 8, 16)
POOL_GROUP_DIM = D_POOL // len(POOL_WINDOWS)
NORM_EPS = 1e-6

T_CHUNK = 64
ROWS = T_CHUNK * BATCH
CONV_HIST = CONV_WIDTH - 1
POOL_HIST = max(POOL_WINDOWS)
T_SUB = 8
VMEM_LIMIT_BYTES = 56 * 1024 * 1024

F32 = jnp.float32
BF16 = jnp.bfloat16


def _sigmoid(v):
    return 0.5 * jnp.tanh(0.5 * v) + 0.5


def _unrolled(n, body, carry):
    for j in range(n):
        carry = body(j, carry)
    return carry


def _mod_kernel(c_ref, w_ref, b_ref, o_ref):
    c = c_ref[...]
    c_act = c * _sigmoid(c)
    o_ref[0, 0] = jnp.dot(c_act, w_ref[0], preferred_element_type=F32) + b_ref[0, 0]


def _modulation(c, ada_w, ada_b):
    depth = ada_w.shape[0]
    return pl.pallas_call(
        _mod_kernel,
        grid=(depth, 3),
        in_specs=[
            pl.BlockSpec((BATCH, D_MODEL), lambda l, j: (0, 0)),
            pl.BlockSpec((1, D_MODEL, D_MODEL), lambda l, j: (l, 0, j)),
            pl.BlockSpec((1, 1, 1, D_MODEL), lambda l, j: (l, j, 0, 0)),
        ],
        out_specs=pl.BlockSpec((1, 1, BATCH, D_MODEL), lambda l, j: (l, j, 0, 0)),
        out_shape=jax.ShapeDtypeStruct((depth, 3, BATCH, D_MODEL), F32),
        name="ada_modulation",
    )(c, ada_w, ada_b.reshape(depth, 3, 1, D_MODEL))


def _layer_kernel(x_ref, mod_ref, pre_g_ref, post_g_ref, w_in_ref, conv_w_ref, conv_b_ref,
                  wg_ref, bg_ref, lam_ref, pool_w_ref, pool_b_ref, pool_s_ref, w_out_ref,
                  o_ref,
                  hb_scr, xr_scr, gr_scr, xp_scr, gp_scr, a_scr, u_scr, ycat_scr, hst_scr):
    step = pl.program_id(0)
    n_sub = T_CHUNK // T_SUB

    @pl.when(step == 0)
    def _():
        xr_scr[0:CONV_HIST] = jnp.zeros((CONV_HIST, BATCH, D_RNN), F32)
        xp_scr[0:POOL_HIST] = jnp.zeros((POOL_HIST, BATCH, D_POOL), F32)
        hst_scr[...] = jnp.zeros((BATCH, D_RNN), F32)

    @pl.when(step > 0)
    def _():
        xr_scr[0:CONV_HIST] = xr_scr[T_CHUNK:T_CHUNK + CONV_HIST]
        xp_scr[0:POOL_HIST] = xp_scr[T_CHUNK:T_CHUNK + POOL_HIST]

    shift = mod_ref[0]
    gmod = pre_g_ref[...] * (1.0 + mod_ref[1])

    def pre_body(j, carry):
        t0 = j * T_SUB
        xs = x_ref[pl.ds(t0, T_SUB)]
        ms = jnp.mean(xs * xs, axis=-1, keepdims=True)
        hh = xs * lax.rsqrt(ms + NORM_EPS) * gmod + shift
        r0 = t0 * BATCH
        hb_scr[pl.ds(r0, T_SUB * BATCH), :] = hh.reshape(T_SUB * BATCH, D_MODEL).astype(BF16)
        return carry

    _unrolled(n_sub,pre_body, 0)

    hb = hb_scr[...]
    xr_scr[CONV_HIST:CONV_HIST + T_CHUNK] = jnp.dot(
        hb, w_in_ref[:, 0:D_RNN], preferred_element_type=F32).reshape(T_CHUNK, BATCH, D_RNN)
    gr_scr[...] = jnp.dot(
        hb, w_in_ref[:, D_RNN:2 * D_RNN], preferred_element_type=F32).reshape(T_CHUNK, BATCH, D_RNN)
    xp_scr[POOL_HIST:POOL_HIST + T_CHUNK] = jnp.dot(
        hb, w_in_ref[:, 2 * D_RNN:2 * D_RNN + D_POOL],
        preferred_element_type=F32).reshape(T_CHUNK, BATCH, D_POOL)
    gp_scr[...] = jnp.dot(
        hb, w_in_ref[:, 2 * D_RNN + D_POOL:], preferred_element_type=F32).reshape(T_CHUNK, BATCH, D_POOL)

    conv_b = conv_b_ref[...]

    def conv_body(j, carry):
        t0 = j * T_SUB
        acc = conv_b + conv_w_ref[0:1, :] * xr_scr[pl.ds(t0, T_SUB)]
        for k in range(1, CONV_WIDTH):
            acc = acc + conv_w_ref[k:k + 1, :] * xr_scr[pl.ds(t0 + k, T_SUB)]
        u_scr[pl.ds(t0, T_SUB)] = acc
        r0 = t0 * BATCH
        hb_scr[pl.ds(r0, T_SUB * BATCH), :] = acc.reshape(T_SUB * BATCH, D_RNN).astype(BF16)
        return carry

    _unrolled(n_sub,conv_body, 0)

    lam = lam_ref[...]
    softplus_neg_lam = jnp.maximum(-lam, 0.0) + jnp.log(1.0 + jnp.exp(-jnp.abs(lam)))
    log_a_scale = -LRU_C * softplus_neg_lam
    for h in range(N_RNN_HEADS):
        sl = slice(h * RNN_HEAD_DIM, (h + 1) * RNN_HEAD_DIM)
        ga = jnp.dot(hb_scr[:, sl], wg_ref[h], preferred_element_type=F32) + bg_ref[h]
        ga = ga.reshape(T_CHUNK, BATCH, 2 * RNN_HEAD_DIM)
        r_gate = _sigmoid(ga[:, :, :RNN_HEAD_DIM])
        i_gate = _sigmoid(ga[:, :, RNN_HEAD_DIM:])
        log_a = r_gate * log_a_scale[:, sl]
        a = jnp.exp(log_a)
        a_scr[:, :, sl] = a
        mult = jnp.sqrt(-jnp.tanh(log_a) * (a * a + 1.0))
        u_scr[:, :, sl] = mult * (i_gate * u_scr[:, :, sl])

    def scan_body(t, h):
        h = a_scr[t] * h + u_scr[t]
        u_scr[t] = h
        return h

    hst_scr[...] = _unrolled(T_CHUNK, scan_body, hst_scr[...])

    def rnn_out_body(j, carry):
        t0 = j * T_SUB
        g = gr_scr[pl.ds(t0, T_SUB)]
        y = u_scr[pl.ds(t0, T_SUB)] * (g * _sigmoid(g))
        r0 = t0 * BATCH
        ycat_scr[pl.ds(r0, T_SUB * BATCH), 0:D_RNN] = y.reshape(T_SUB * BATCH, D_RNN).astype(BF16)
        return carry

    _unrolled(n_sub,rnn_out_body, 0)

    t_base = step * T_CHUNK
    for g, win in enumerate(POOL_WINDOWS):
        sl = slice(g * POOL_GROUP_DIM, (g + 1) * POOL_GROUP_DIM)

        def pool_body(j, carry, sl=sl, win=win):
            t0 = j * T_SUB
            xv = xp_scr[pl.ds(t0 + POOL_HIST - (win - 1), T_SUB + win - 1), :, sl]
            s = xv
            d = 1
            while d < win:
                s = s[d:] + s[:-d]
                d *= 2
            t_glob = t_base + t0 + lax.broadcasted_iota(jnp.int32, (T_SUB, 1, 1), 0)
            inv_count = 1.0 / jnp.minimum(t_glob + 1, win).astype(F32)
            pooled = s * inv_count - xv[win - 1:]
            r0 = t0 * BATCH
            hb_scr[pl.ds(r0, T_SUB * BATCH), sl] = pooled.reshape(
                T_SUB * BATCH, POOL_GROUP_DIM).astype(BF16)
            return carry

        _unrolled(n_sub,pool_body, 0)

        yp = jnp.dot(hb_scr[:, sl], pool_w_ref[g], preferred_element_type=F32) + pool_b_ref[g]
        yp = yp.reshape(T_CHUNK, BATCH, POOL_GROUP_DIM) * pool_s_ref[:, sl]
        gp = gp_scr[:, :, sl]
        yp = yp * (gp * _sigmoid(gp))
        ycat_scr[:, D_RNN + g * POOL_GROUP_DIM:D_RNN + (g + 1) * POOL_GROUP_DIM] = yp.reshape(
            ROWS, POOL_GROUP_DIM).astype(BF16)

    a_scr[...] = jnp.dot(ycat_scr[...], w_out_ref[...],
                         preferred_element_type=F32).reshape(T_CHUNK, BATCH, D_MODEL)
    gate_g = mod_ref[2] * post_g_ref[...]

    def post_body(j, carry):
        t0 = j * T_SUB
        y = a_scr[pl.ds(t0, T_SUB)]
        ms = jnp.mean(y * y, axis=-1, keepdims=True)
        o_ref[pl.ds(t0, T_SUB)] = x_ref[pl.ds(t0, T_SUB)] + y * lax.rsqrt(ms + NORM_EPS) * gate_g
        return carry

    _unrolled(n_sub,post_body, 0)


def _const_spec(shape):
    zeros = (0,) * len(shape)
    return pl.BlockSpec(shape, lambda i: zeros, pipeline_mode=pl.Buffered(1))


def _layer(xt, mod, pre_g, post_g, w_in, conv_w, conv_b, wg, bg, lam, pool_w, pool_b, pool_s, w_out):
    seq = xt.shape[0]
    d_mix = D_RNN + D_POOL
    act_spec = pl.BlockSpec((T_CHUNK, BATCH, D_MODEL), lambda i: (i, 0, 0))
    return pl.pallas_call(
        _layer_kernel,
        grid=(seq // T_CHUNK,),
        in_specs=[
            act_spec,
            _const_spec((3, BATCH, D_MODEL)),
            _const_spec((1, D_MODEL)),
            _const_spec((1, D_MODEL)),
            _const_spec((D_MODEL, 2 * d_mix)),
            _const_spec((CONV_WIDTH, D_RNN)),
            _const_spec((1, D_RNN)),
            _const_spec((N_RNN_HEADS, RNN_HEAD_DIM, 2 * RNN_HEAD_DIM)),
            _const_spec((N_RNN_HEADS, 1, 2 * RNN_HEAD_DIM)),
            _const_spec((1, D_RNN)),
            _const_spec((len(POOL_WINDOWS), POOL_GROUP_DIM, POOL_GROUP_DIM)),
            _const_spec((len(POOL_WINDOWS), 1, POOL_GROUP_DIM)),
            _const_spec((1, D_POOL)),
            _const_spec((d_mix, D_MODEL)),
        ],
        out_specs=act_spec,
        out_shape=jax.ShapeDtypeStruct(xt.shape, F32),
        scratch_shapes=[
            pltpu.VMEM((ROWS, D_MODEL), BF16),
            pltpu.VMEM((CONV_HIST + T_CHUNK, BATCH, D_RNN), F32),
            pltpu.VMEM((T_CHUNK, BATCH, D_RNN), F32),
            pltpu.VMEM((POOL_HIST + T_CHUNK, BATCH, D_POOL), F32),
            pltpu.VMEM((T_CHUNK, BATCH, D_POOL), F32),
            pltpu.VMEM((T_CHUNK, BATCH, D_RNN), F32),
            pltpu.VMEM((T_CHUNK, BATCH, D_RNN), F32),
            pltpu.VMEM((ROWS, d_mix), BF16),
            pltpu.VMEM((BATCH, D_RNN), F32),
        ],
        compiler_params=pltpu.CompilerParams(
            dimension_semantics=("arbitrary",),
            vmem_limit_bytes=VMEM_LIMIT_BYTES,
        ),
        name="hybrid_layer",
    )(xt, mod, pre_g, post_g, w_in, conv_w, conv_b, wg, bg, lam, pool_w, pool_b, pool_s, w_out)


def kernel(x, c, ada_w, ada_b, pre_norm_g, w_in, conv_w, conv_b, gate_a_w, gate_a_b, gate_x_w,
           gate_x_b, lru_lambda, pool_w, pool_b, pool_scale, w_out, post_norm_g):
    depth = ada_w.shape[0]
    mod = _modulation(c, ada_w, ada_b)
    xt = jnp.transpose(x, (1, 0, 2))
    for l in range(depth):
        wg = jnp.concatenate([gate_a_w[l], gate_x_w[l]], axis=-1).astype(BF16)
        bg = jnp.concatenate([gate_a_b[l], gate_x_b[l]], axis=-1)[:, None, :]
        xt = _layer(
            xt, mod[l], pre_norm_g[l][None, :], post_norm_g[l][None, :],
            w_in[l].astype(BF16), conv_w[l], conv_b[l][None, :], wg, bg,
            lru_lambda[l][None, :], pool_w[l].astype(BF16), pool_b[l][:, None, :],
            pool_scale[l][None, :], w_out[l].astype(BF16))
    return jnp.transpose(xt, (1, 0, 2))
```

```python
import jax
import jax.numpy as jnp
from jax import lax
from jax.experimental import pallas as pl
from jax.experimental.pallas import tpu as pltpu

D_MODEL = 1024
BATCH = 8
D_RNN = D_MODEL
D_POOL = D_MODEL
D_MIX = D_RNN + D_POOL
N_RNN_HEADS = 8
RNN_HEAD_DIM = D_RNN // N_RNN_HEADS
CONV_WIDTH = 4
LRU_C = 8.0
POOL_WINDOWS = (2, 4, 8, 16)
POOL_GROUP_DIM = D_POOL // len(POOL_WINDOWS)
NORM_EPS = 1e-6

T_CHUNK = 64
ROWS = T_CHUNK * BATCH
CONV_HIST = CONV_WIDTH - 1
POOL_HIST = max(POOL_WINDOWS)
T_SUB = 8
N_SUB = T_CHUNK // T_SUB
VMEM_LIMIT_BYTES = 56 * 1024 * 1024

V_PRE_G, V_POST_G, V_CONV_B, V_LAMBDA, V_POOL_B, V_POOL_S, V_CONV_W = 0, 1, 2, 3, 4, 5, 6
N_VEC = V_CONV_W + CONV_WIDTH

F32 = jnp.float32
BF16 = jnp.bfloat16


def _mod_kernel(c_ref, w_ref, b_ref, o_ref):
    c = c_ref[...]
    c_act = c * (0.5 * jnp.tanh(0.5 * c) + 0.5)
    o_ref[0, 0] = jnp.dot(c_act, w_ref[0], preferred_element_type=F32) + b_ref[0, 0]


def _modulation(c, ada_w, ada_b):
    depth = ada_w.shape[0]
    return pl.pallas_call(
        _mod_kernel,
        grid=(depth, 3),
        in_specs=[
            pl.BlockSpec((BATCH, D_MODEL), lambda l, j: (0, 0)),
            pl.BlockSpec((1, D_MODEL, D_MODEL), lambda l, j: (l, 0, j)),
            pl.BlockSpec((1, 1, 1, D_MODEL), lambda l, j: (l, j, 0, 0)),
        ],
        out_specs=pl.BlockSpec((1, 1, BATCH, D_MODEL), lambda l, j: (l, j, 0, 0)),
        out_shape=jax.ShapeDtypeStruct((depth, 3, BATCH, D_MODEL), F32),
        name="ada_modulation",
    )(c, ada_w, ada_b.reshape(depth, 3, 1, D_MODEL))


def _rows(t0, n_t):
    return pl.ds(t0 * BATCH, n_t * BATCH)


def _layer_kernel(x_ref, mod_ref, vec_ref, w_in_ref, wg_ref, bg_ref, pool_w_ref, w_out_ref,
                  o_ref,
                  hb_scr, ucb_scr, plb_scr, xr_scr, gr_scr, xp_scr, gp_scr, a_scr, u_scr,
                  ycat_scr, hst_scr):
    step = pl.program_id(0)

    @pl.when(step == 0)
    def _():
        xr_scr[0:CONV_HIST] = jnp.zeros((CONV_HIST, BATCH, D_RNN), F32)
        xp_scr[0:POOL_HIST] = jnp.zeros((POOL_HIST, BATCH, D_POOL), F32)
        hst_scr[...] = jnp.zeros((BATCH, D_RNN), F32)

    @pl.when(step > 0)
    def _():
        xr_scr[0:CONV_HIST] = xr_scr[T_CHUNK:T_CHUNK + CONV_HIST]
        xp_scr[0:POOL_HIST] = xp_scr[T_CHUNK:T_CHUNK + POOL_HIST]

    shift = mod_ref[0]
    gmod = vec_ref[V_PRE_G] * (1.0 + mod_ref[1])
    for j in range(N_SUB):
        t0 = j * T_SUB
        xs = x_ref[pl.ds(t0, T_SUB)]
        ms = jnp.mean(xs * xs, axis=-1, keepdims=True)
        hh = xs * lax.rsqrt(ms + NORM_EPS) * gmod + shift
        hb_scr[_rows(t0, T_SUB), :] = hh.reshape(T_SUB * BATCH, D_MODEL).astype(BF16)

    hb = hb_scr[...]

    def proj(col0, width):
        return jnp.dot(hb, w_in_ref[:, col0:col0 + width],
                       preferred_element_type=F32).reshape(T_CHUNK, BATCH, width)

    xr_scr[CONV_HIST:CONV_HIST + T_CHUNK] = proj(0, D_RNN)
    for j in range(N_SUB):
        t0 = j * T_SUB
        acc = vec_ref[V_CONV_B] + vec_ref[V_CONV_W] * xr_scr[pl.ds(t0, T_SUB)]
        for k in range(1, CONV_WIDTH):
            acc = acc + vec_ref[V_CONV_W + k] * xr_scr[pl.ds(t0 + k, T_SUB)]
        u_scr[pl.ds(t0, T_SUB)] = acc
        ucb_scr[_rows(t0, T_SUB), :] = acc.reshape(T_SUB * BATCH, D_RNN).astype(BF16)

    xp_scr[POOL_HIST:POOL_HIST + T_CHUNK] = proj(2 * D_RNN, D_POOL)
    lam = vec_ref[V_LAMBDA]
    softplus_neg_lam = jnp.maximum(-lam, 0.0) + jnp.log(1.0 + jnp.exp(-jnp.abs(lam)))
    half_log_a_scale = (-0.5 * LRU_C) * softplus_neg_lam
    for h in range(N_RNN_HEADS):
        sl = slice(h * RNN_HEAD_DIM, (h + 1) * RNN_HEAD_DIM)
        pre = jnp.dot(ucb_scr[:, sl], wg_ref[h], preferred_element_type=F32)
        pre = pre.reshape(T_CHUNK, BATCH, 2 * RNN_HEAD_DIM) + bg_ref[:, h * 2 * RNN_HEAD_DIM:
                                                                     (h + 1) * 2 * RNN_HEAD_DIM]
        tanh_r = jnp.tanh(pre[:, :, :RNN_HEAD_DIM])
        tanh_i = jnp.tanh(pre[:, :, RNN_HEAD_DIM:])
        hls = half_log_a_scale[:, sl]
        log_a = tanh_r * hls + hls
        a = jnp.exp(log_a)
        a_scr[:, :, sl] = a
        one_minus_a2 = jnp.tanh(log_a) * (-1.0 - a * a)
        mult = jnp.where(one_minus_a2 > 0.0, one_minus_a2 * lax.rsqrt(one_minus_a2), 0.0)
        u_scr[:, :, sl] = mult * ((tanh_i + 1.0) * u_scr[:, :, sl])

    gp_scr[...] = proj(2 * D_RNN + D_POOL, D_POOL)
    t_base = step * T_CHUNK
    for g, win in enumerate(POOL_WINDOWS):
        sl = slice(g * POOL_GROUP_DIM, (g + 1) * POOL_GROUP_DIM)
        for j in range(N_SUB):
            t0 = j * T_SUB
            xv = xp_scr[pl.ds(t0 + POOL_HIST - (win - 1), T_SUB + win - 1), :, sl]
            s = xv
            d = 1
            while d < win:
                s = s[d:] + s[:-d]
                d *= 2
            if t0 < win:
                t_glob = t_base + t0 + lax.broadcasted_iota(jnp.int32, (T_SUB, 1, 1), 0)
                inv_count = 1.0 / jnp.minimum(t_glob + 1, win).astype(F32)
            else:
                inv_count = 1.0 / win
            pooled = s * inv_count - xv[win - 1:]
            plb_scr[_rows(t0, T_SUB), sl] = pooled.reshape(
                T_SUB * BATCH, POOL_GROUP_DIM).astype(BF16)

        yp = jnp.dot(plb_scr[:, sl], pool_w_ref[g], preferred_element_type=F32)
        yp = yp.reshape(T_CHUNK, BATCH, POOL_GROUP_DIM)
        yp = (yp + vec_ref[V_POOL_B, :, sl]) * vec_ref[V_POOL_S, :, sl]
        hg = gp_scr[:, :, sl]
        yp = yp * (hg * (jnp.tanh(hg) + 1.0))
        ycat_scr[:, D_RNN + g * POOL_GROUP_DIM:D_RNN + (g + 1) * POOL_GROUP_DIM] = yp.reshape(
            ROWS, POOL_GROUP_DIM).astype(BF16)

    gr_scr[...] = proj(D_RNN, D_RNN)
    h_state = hst_scr[...]
    for t in range(T_CHUNK):
        h_state = a_scr[t] * h_state + u_scr[t]
        u_scr[t] = h_state
    hst_scr[...] = h_state

    for j in range(N_SUB):
        t0 = j * T_SUB
        hg = gr_scr[pl.ds(t0, T_SUB)]
        y = u_scr[pl.ds(t0, T_SUB)] * (hg * (jnp.tanh(hg) + 1.0))
        ycat_scr[_rows(t0, T_SUB), 0:D_RNN] = y.reshape(T_SUB * BATCH, D_RNN).astype(BF16)

    y_out = jnp.dot(ycat_scr[:, D_RNN:], w_out_ref[D_RNN:, :], preferred_element_type=F32)
    y_out = jnp.dot(ycat_scr[:, :D_RNN], w_out_ref[:D_RNN, :], preferred_element_type=F32) + y_out
    a_scr[...] = y_out.reshape(T_CHUNK, BATCH, D_MODEL)
    gate_g = mod_ref[2] * vec_ref[V_POST_G]
    for j in range(N_SUB):
        t0 = j * T_SUB
        y = a_scr[pl.ds(t0, T_SUB)]
        ms = jnp.mean(y * y, axis=-1, keepdims=True)
        o_ref[pl.ds(t0, T_SUB)] = x_ref[pl.ds(t0, T_SUB)] + y * lax.rsqrt(ms + NORM_EPS) * gate_g


def _layer(xt, layer, mod, vecs, w_in, wg, bg, pool_w, w_out):
    seq = xt.shape[0]

    def param_spec(arr):
        block = (None,) + arr.shape[1:]
        zeros = (0,) * (arr.ndim - 1)
        return pl.BlockSpec(block, lambda i: (layer,) + zeros, pipeline_mode=pl.Buffered(1))

    act_spec = pl.BlockSpec((T_CHUNK, BATCH, D_MODEL), lambda i: (i, 0, 0))
    params = (mod, vecs, w_in, wg, bg, pool_w, w_out)
    return pl.pallas_call(
        _layer_kernel,
        grid=(seq // T_CHUNK,),
        in_specs=[act_spec] + [param_spec(p) for p in params],
        out_specs=act_spec,
        out_shape=jax.ShapeDtypeStruct(xt.shape, F32),
        scratch_shapes=[
            pltpu.VMEM((ROWS, D_MODEL), BF16),
            pltpu.VMEM((ROWS, D_RNN), BF16),
            pltpu.VMEM((ROWS, D_POOL), BF16),
            pltpu.VMEM((CONV_HIST + T_CHUNK, BATCH, D_RNN), F32),
            pltpu.VMEM((T_CHUNK, BATCH, D_RNN), F32),
            pltpu.VMEM((POOL_HIST + T_CHUNK, BATCH, D_POOL), F32),
            pltpu.VMEM((T_CHUNK, BATCH, D_POOL), F32),
            pltpu.VMEM((T_CHUNK, BATCH, D_RNN), F32),
            pltpu.VMEM((T_CHUNK, BATCH, D_RNN), F32),
            pltpu.VMEM((ROWS, D_MIX), BF16),
            pltpu.VMEM((BATCH, D_RNN), F32),
        ],
        compiler_params=pltpu.CompilerParams(
            dimension_semantics=("arbitrary",),
            vmem_limit_bytes=VMEM_LIMIT_BYTES,
        ),
        name="hybrid_layer",
    )(xt, *params)


def _sublane_bcast(v):
    return jnp.broadcast_to(v[..., None, :], v.shape[:-1] + (BATCH, v.shape[-1]))


def kernel(x, c, ada_w, ada_b, pre_norm_g, w_in, conv_w, conv_b, gate_a_w, gate_a_b, gate_x_w,
           gate_x_b, lru_lambda, pool_w, pool_b, pool_scale, w_out, post_norm_g):
    depth = ada_w.shape[0]
    mod = _modulation(c, ada_w, ada_b)

    col_scale = jnp.concatenate([jnp.ones((D_RNN,), F32), jnp.full((D_RNN,), 0.5, F32),
                                 jnp.ones((D_POOL,), F32), jnp.full((D_POOL,), 0.5, F32)])
    w_in_b = (w_in * col_scale).astype(BF16)
    w_out_b = w_out.astype(BF16)
    pool_w_b = pool_w.astype(BF16)
    wg = jnp.concatenate([gate_a_w, gate_x_w], axis=-1).astype(BF16)
    bg = 0.5 * jnp.concatenate([gate_a_b, gate_x_b], axis=-1)
    bg = _sublane_bcast(bg.reshape(depth, N_RNN_HEADS * 2 * RNN_HEAD_DIM))
    vecs = jnp.concatenate([
        jnp.stack([pre_norm_g, post_norm_g, 0.5 * conv_b, lru_lambda,
                   pool_b.reshape(depth, D_POOL), pool_scale], axis=1),
        0.5 * conv_w], axis=1)
    vecs = _sublane_bcast(vecs)

    xt = jnp.transpose(x, (1, 0, 2))
    for layer in range(depth):
        xt = _layer(xt, layer, mod, vecs, w_in_b, wg, bg, pool_w_b, w_out_b)
    return jnp.transpose(xt, (1, 0, 2))
```

```python
import functools

import jax
import jax.numpy as jnp
from jax import lax
from jax.experimental import pallas as pl
from jax.experimental.pallas import tpu as pltpu

D_MODEL = 1024
BATCH = 8
D_RNN = D_MODEL
D_POOL = D_MODEL
D_MIX = D_RNN + D_POOL
N_RNN_HEADS = 8
RNN_HEAD_DIM = D_RNN // N_RNN_HEADS
CONV_WIDTH = 4
LRU_C = 8.0
POOL_WINDOWS = (2, 4, 8, 16)
POOL_GROUP_DIM = D_POOL // len(POOL_WINDOWS)
NORM_EPS = 1e-6

T_CHUNK = 64
ROWS = T_CHUNK * BATCH
CONV_HIST = CONV_WIDTH - 1
POOL_HIST = max(POOL_WINDOWS)
T_SUB = 8
N_SUB = T_CHUNK // T_SUB
VMEM_LIMIT_BYTES = 56 * 1024 * 1024

V_PRE_G, V_POST_G, V_CONV_B, V_LAMBDA, V_POOL_B, V_POOL_S, V_CONV_W = 0, 1, 2, 3, 4, 5, 6
N_VEC = V_CONV_W + CONV_WIDTH

F32 = jnp.float32
BF16 = jnp.bfloat16


def _mod_kernel(c_ref, w_ref, b_ref, o_ref):
    c = c_ref[...]
    c_act = c * (0.5 * jnp.tanh(0.5 * c) + 0.5)
    o_ref[0, 0] = jnp.dot(c_act, w_ref[0], preferred_element_type=F32) + b_ref[0, 0]


def _modulation(c, ada_w, ada_b):
    depth = ada_w.shape[0]
    return pl.pallas_call(
        _mod_kernel,
        grid=(depth, 3),
        in_specs=[
            pl.BlockSpec((BATCH, D_MODEL), lambda l, j: (0, 0)),
            pl.BlockSpec((1, D_MODEL, D_MODEL), lambda l, j: (l, 0, j)),
            pl.BlockSpec((1, 1, 1, D_MODEL), lambda l, j: (l, j, 0, 0)),
        ],
        out_specs=pl.BlockSpec((1, 1, BATCH, D_MODEL), lambda l, j: (l, j, 0, 0)),
        out_shape=jax.ShapeDtypeStruct((depth, 3, BATCH, D_MODEL), F32),
        name="ada_modulation",
    )(c, ada_w, ada_b.reshape(depth, 3, 1, D_MODEL))


def _rows(t0, n_t):
    return pl.ds(t0 * BATCH, n_t * BATCH)


def _layer_kernel(natural_in, natural_out, n_steps,
                  x_ref, mod_ref, vec_ref, w_in_ref, wg_ref, bg_ref, pool_w_ref, w_out_ref,
                  o_ref,
                  hb_scr, ucb_scr, plb_scr, xr_scr, gr_scr, xp_scr, gp_scr, a_scr, u_scr,
                  ycat_scr, hst_scr, *io_scr):
    step = pl.program_id(0)
    slot = lax.rem(step, 2)
    io_scr = list(io_scr)

    def relayout_copies(hbm_ref, vmem_ref, sem_ref, chunk, slot_, to_vmem):
        t0 = pl.multiple_of(chunk * T_CHUNK, T_CHUNK)
        copies = []
        for b in range(BATCH):
            hbm = hbm_ref.at[b, pl.ds(t0, T_CHUNK), :]
            vmem = vmem_ref.at[slot_, :, b, :]
            src, dst = (hbm, vmem) if to_vmem else (vmem, hbm)
            copies.append(pltpu.make_async_copy(src, dst, sem_ref.at[slot_]))
        return copies

    if natural_in:
        xin_scr, in_sem = io_scr[:2]
        io_scr = io_scr[2:]
        in_copies = functools.partial(relayout_copies, x_ref, xin_scr, in_sem, to_vmem=True)

        @pl.when(step == 0)
        def _():
            for cp in in_copies(step, slot):
                cp.start()

        for cp in in_copies(step, slot):
            cp.wait()

        @pl.when(step + 1 < n_steps)
        def _():
            for cp in in_copies(step + 1, 1 - slot):
                cp.start()

        x_cur = xin_scr.at[slot]
    else:
        x_cur = x_ref

    if natural_out:
        xout_scr, out_sem = io_scr[:2]
        out_copies = functools.partial(relayout_copies, o_ref, xout_scr, out_sem, to_vmem=False)

        @pl.when(step >= 2)
        def _():
            for cp in out_copies(step - 2, slot):
                cp.wait()

        o_cur = xout_scr.at[slot]
    else:
        o_cur = o_ref

    @pl.when(step == 0)
    def _():
        xr_scr[0:CONV_HIST] = jnp.zeros((CONV_HIST, BATCH, D_RNN), F32)
        xp_scr[0:POOL_HIST] = jnp.zeros((POOL_HIST, BATCH, D_POOL), F32)
        hst_scr[...] = jnp.zeros((BATCH, D_RNN), F32)

    @pl.when(step > 0)
    def _():
        xr_scr[0:CONV_HIST] = xr_scr[T_CHUNK:T_CHUNK + CONV_HIST]
        xp_scr[0:POOL_HIST] = xp_scr[T_CHUNK:T_CHUNK + POOL_HIST]

    shift = mod_ref[0]
    gmod = vec_ref[V_PRE_G] * (1.0 + mod_ref[1])
    for j in range(N_SUB):
        t0 = j * T_SUB
        xs = x_cur[pl.ds(t0, T_SUB)]
        ms = jnp.mean(xs * xs, axis=-1, keepdims=True)
        hh = xs * lax.rsqrt(ms + NORM_EPS) * gmod + shift
        hb_scr[_rows(t0, T_SUB), :] = hh.reshape(T_SUB * BATCH, D_MODEL).astype(BF16)

    hb = hb_scr[...]

    def proj(col0, width):
        return jnp.dot(hb, w_in_ref[:, col0:col0 + width],
                       preferred_element_type=F32).reshape(T_CHUNK, BATCH, width)

    xr_scr[CONV_HIST:CONV_HIST + T_CHUNK] = proj(0, D_RNN)
    for j in range(N_SUB):
        t0 = j * T_SUB
        acc = vec_ref[V_CONV_B] + vec_ref[V_CONV_W] * xr_scr[pl.ds(t0, T_SUB)]
        for k in range(1, CONV_WIDTH):
            acc = acc + vec_ref[V_CONV_W + k] * xr_scr[pl.ds(t0 + k, T_SUB)]
        u_scr[pl.ds(t0, T_SUB)] = acc
        ucb_scr[_rows(t0, T_SUB), :] = acc.reshape(T_SUB * BATCH, D_RNN).astype(BF16)

    xp_scr[POOL_HIST:POOL_HIST + T_CHUNK] = proj(2 * D_RNN, D_POOL)
    lam = vec_ref[V_LAMBDA]
    softplus_neg_lam = jnp.maximum(-lam, 0.0) + jnp.log(1.0 + jnp.exp(-jnp.abs(lam)))
    half_log_a_scale = (-0.5 * LRU_C) * softplus_neg_lam
    for h in range(N_RNN_HEADS):
        sl = slice(h * RNN_HEAD_DIM, (h + 1) * RNN_HEAD_DIM)
        pre = jnp.dot(ucb_scr[:, sl], wg_ref[h], preferred_element_type=F32)
        pre = pre.reshape(T_CHUNK, BATCH, 2 * RNN_HEAD_DIM) + bg_ref[:, h * 2 * RNN_HEAD_DIM:
                                                                     (h + 1) * 2 * RNN_HEAD_DIM]
        tanh_r = jnp.tanh(pre[:, :, :RNN_HEAD_DIM])
        tanh_i = jnp.tanh(pre[:, :, RNN_HEAD_DIM:])
        hls = half_log_a_scale[:, sl]
        log_a = tanh_r * hls + hls
        a = jnp.exp(log_a)
        a_scr[:, :, sl] = a
        one_minus_a2 = jnp.tanh(log_a) * (-1.0 - a * a)
        mult = jnp.where(one_minus_a2 > 0.0, one_minus_a2 * lax.rsqrt(one_minus_a2), 0.0)
        u_scr[:, :, sl] = mult * ((tanh_i + 1.0) * u_scr[:, :, sl])

    gp_scr[...] = proj(2 * D_RNN + D_POOL, D_POOL)
    t_base = step * T_CHUNK
    for g, win in enumerate(POOL_WINDOWS):
        sl = slice(g * POOL_GROUP_DIM, (g + 1) * POOL_GROUP_DIM)
        for j in range(N_SUB):
            t0 = j * T_SUB
            xv = xp_scr[pl.ds(t0 + POOL_HIST - (win - 1), T_SUB + win - 1), :, sl]
            s = xv
            d = 1
            while d < win:
                s = s[d:] + s[:-d]
                d *= 2
            if t0 < win:
                t_glob = t_base + t0 + lax.broadcasted_iota(jnp.int32, (T_SUB, 1, 1), 0)
                inv_count = 1.0 / jnp.minimum(t_glob + 1, win).astype(F32)
            else:
                inv_count = 1.0 / win
            pooled = s * inv_count - xv[win - 1:]
            plb_scr[_rows(t0, T_SUB), sl] = pooled.reshape(
                T_SUB * BATCH, POOL_GROUP_DIM).astype(BF16)

        yp = jnp.dot(plb_scr[:, sl], pool_w_ref[g], preferred_element_type=F32)
        yp = yp.reshape(T_CHUNK, BATCH, POOL_GROUP_DIM)
        yp = (yp + vec_ref[V_POOL_B, :, sl]) * vec_ref[V_POOL_S, :, sl]
        hg = gp_scr[:, :, sl]
        yp = yp * (hg * (jnp.tanh(hg) + 1.0))
        ycat_scr[:, D_RNN + g * POOL_GROUP_DIM:D_RNN + (g + 1) * POOL_GROUP_DIM] = yp.reshape(
            ROWS, POOL_GROUP_DIM).astype(BF16)

    gr_scr[...] = proj(D_RNN, D_RNN)
    h_state = hst_scr[...]
    for t in range(T_CHUNK):
        h_state = a_scr[t] * h_state + u_scr[t]
        u_scr[t] = h_state
    hst_scr[...] = h_state

    for j in range(N_SUB):
        t0 = j * T_SUB
        hg = gr_scr[pl.ds(t0, T_SUB)]
        y = u_scr[pl.ds(t0, T_SUB)] * (hg * (jnp.tanh(hg) + 1.0))
        ycat_scr[_rows(t0, T_SUB), 0:D_RNN] = y.reshape(T_SUB * BATCH, D_RNN).astype(BF16)

    y_out = jnp.dot(ycat_scr[:, D_RNN:], w_out_ref[D_RNN:, :], preferred_element_type=F32)
    y_out = jnp.dot(ycat_scr[:, :D_RNN], w_out_ref[:D_RNN, :], preferred_element_type=F32) + y_out
    a_scr[...] = y_out.reshape(T_CHUNK, BATCH, D_MODEL)
    gate_g = mod_ref[2] * vec_ref[V_POST_G]
    for j in range(N_SUB):
        t0 = j * T_SUB
        y = a_scr[pl.ds(t0, T_SUB)]
        ms = jnp.mean(y * y, axis=-1, keepdims=True)
        o_cur[pl.ds(t0, T_SUB)] = x_cur[pl.ds(t0, T_SUB)] + y * lax.rsqrt(ms + NORM_EPS) * gate_g

    if natural_out:
        for cp in out_copies(step, slot):
            cp.start()

        @pl.when(step == n_steps - 1)
        def _():
            if n_steps >= 2:
                for cp in out_copies(step - 1, 1 - slot):
                    cp.wait()
            for cp in out_copies(step, slot):
                cp.wait()


def _layer(x_in, layer, natural_in, natural_out, mod, vecs, w_in, wg, bg, pool_w, w_out):
    seq = x_in.shape[1] if natural_in else x_in.shape[0]
    n_steps = seq // T_CHUNK

    def param_spec(arr):
        block = (None,) + arr.shape[1:]
        zeros = (0,) * (arr.ndim - 1)
        return pl.BlockSpec(block, lambda i: (layer,) + zeros, pipeline_mode=pl.Buffered(1))

    act_spec = pl.BlockSpec((T_CHUNK, BATCH, D_MODEL), lambda i: (i, 0, 0))
    hbm_spec = pl.BlockSpec(memory_space=pl.ANY)
    io_slots = [pltpu.VMEM((2, T_CHUNK, BATCH, D_MODEL), F32), pltpu.SemaphoreType.DMA((2,))]
    out_shape = (BATCH, seq, D_MODEL) if natural_out else (seq, BATCH, D_MODEL)
    params = (mod, vecs, w_in, wg, bg, pool_w, w_out)
    return pl.pallas_call(
        functools.partial(_layer_kernel, natural_in, natural_out, n_steps),
        grid=(n_steps,),
        in_specs=[hbm_spec if natural_in else act_spec] + [param_spec(p) for p in params],
        out_specs=hbm_spec if natural_out else act_spec,
        out_shape=jax.ShapeDtypeStruct(out_shape, F32),
        scratch_shapes=[
            pltpu.VMEM((ROWS, D_MODEL), BF16),
            pltpu.VMEM((ROWS, D_RNN), BF16),
            pltpu.VMEM((ROWS, D_POOL), BF16),
            pltpu.VMEM((CONV_HIST + T_CHUNK, BATCH, D_RNN), F32),
            pltpu.VMEM((T_CHUNK, BATCH, D_RNN), F32),
            pltpu.VMEM((POOL_HIST + T_CHUNK, BATCH, D_POOL), F32),
            pltpu.VMEM((T_CHUNK, BATCH, D_POOL), F32),
            pltpu.VMEM((T_CHUNK, BATCH, D_RNN), F32),
            pltpu.VMEM((T_CHUNK, BATCH, D_RNN), F32),
            pltpu.VMEM((ROWS, D_MIX), BF16),
            pltpu.VMEM((BATCH, D_RNN), F32),
        ] + (io_slots if natural_in else []) + (io_slots if natural_out else []),
        compiler_params=pltpu.CompilerParams(
            dimension_semantics=("arbitrary",),
            vmem_limit_bytes=VMEM_LIMIT_BYTES,
        ),
        name="hybrid_layer",
    )(x_in, *params)


def _sublane_bcast(v):
    return jnp.broadcast_to(v[..., None, :], v.shape[:-1] + (BATCH, v.shape[-1]))


def kernel(x, c, ada_w, ada_b, pre_norm_g, w_in, conv_w, conv_b, gate_a_w, gate_a_b, gate_x_w,
           gate_x_b, lru_lambda, pool_w, pool_b, pool_scale, w_out, post_norm_g):
    depth = ada_w.shape[0]
    mod = _modulation(c, ada_w, ada_b)

    col_scale = jnp.concatenate([jnp.ones((D_RNN,), F32), jnp.full((D_RNN,), 0.5, F32),
                                 jnp.ones((D_POOL,), F32), jnp.full((D_POOL,), 0.5, F32)])
    w_in_b = (w_in * col_scale).astype(BF16)
    w_out_b = w_out.astype(BF16)
    pool_w_b = pool_w.astype(BF16)
    wg = jnp.concatenate([gate_a_w, gate_x_w], axis=-1).astype(BF16)
    bg = 0.5 * jnp.concatenate([gate_a_b, gate_x_b], axis=-1)
    bg = _sublane_bcast(bg.reshape(depth, N_RNN_HEADS * 2 * RNN_HEAD_DIM))
    vecs = jnp.concatenate([
        jnp.stack([pre_norm_g, post_norm_g, 0.5 * conv_b, lru_lambda,
                   pool_b.reshape(depth, D_POOL), pool_scale], axis=1),
        0.5 * conv_w], axis=1)
    vecs = _sublane_bcast(vecs)

    h = x
    for layer in range(depth):
        h = _layer(h, layer, layer == 0, layer == depth - 1,
                   mod, vecs, w_in_b, wg, bg, pool_w_b, w_out_b)
    return h
```

```python
import functools

import jax
import jax.numpy as jnp
from jax import lax
from jax.experimental import pallas as pl
from jax.experimental.pallas import tpu as pltpu

D_MODEL = 1024
BATCH = 8
D_RNN = D_MODEL
D_POOL = D_MODEL
D_MIX = D_RNN + D_POOL
N_RNN_HEADS = 8
RNN_HEAD_DIM = D_RNN // N_RNN_HEADS
CONV_WIDTH = 4
LRU_C = 8.0
POOL_WINDOWS = (2, 4, 8, 16)
POOL_GROUP_DIM = D_POOL // len(POOL_WINDOWS)
NORM_EPS = 1e-6

T_CHUNK = 64
CHUNKS_PER_STEP = 2
T_STEP = T_CHUNK * CHUNKS_PER_STEP
ROWS = T_CHUNK * BATCH
CONV_HIST = CONV_WIDTH - 1
POOL_HIST = max(POOL_WINDOWS)
T_SUB = 8
N_SUB = T_CHUNK // T_SUB
VMEM_LIMIT_BYTES = 56 * 1024 * 1024

V_PRE_G, V_POST_G, V_CONV_B, V_LAMBDA, V_POOL_B, V_POOL_S, V_CONV_W = 0, 1, 2, 3, 4, 5, 6
N_VEC = V_CONV_W + CONV_WIDTH

F32 = jnp.float32
BF16 = jnp.bfloat16


def _mod_kernel(c_ref, w_ref, b_ref, o_ref):
    c = c_ref[...]
    c_act = c * (0.5 * jnp.tanh(0.5 * c) + 0.5)
    o_ref[0, 0] = jnp.dot(c_act, w_ref[0], preferred_element_type=F32) + b_ref[0, 0]


def _modulation(c, ada_w, ada_b):
    depth = ada_w.shape[0]
    return pl.pallas_call(
        _mod_kernel,
        grid=(depth, 3),
        in_specs=[
            pl.BlockSpec((BATCH, D_MODEL), lambda l, j: (0, 0)),
            pl.BlockSpec((1, D_MODEL, D_MODEL), lambda l, j: (l, 0, j)),
            pl.BlockSpec((1, 1, 1, D_MODEL), lambda l, j: (l, j, 0, 0)),
        ],
        out_specs=pl.BlockSpec((1, 1, BATCH, D_MODEL), lambda l, j: (l, j, 0, 0)),
        out_shape=jax.ShapeDtypeStruct((depth, 3, BATCH, D_MODEL), F32),
        name="ada_modulation",
    )(c, ada_w, ada_b.reshape(depth, 3, 1, D_MODEL))


def _rows(t0, n_t):
    return pl.ds(t0 * BATCH, n_t * BATCH)


def _layer_kernel(natural_in, natural_out, n_steps,
                  x_ref, mod_ref, vec_ref, w_in_ref, wg_ref, bg_ref, pool_w_ref, w_out_ref,
                  o_ref,
                  hb_scr, ucb_scr, plb_scr, xr_scr, gr_scr, xp_scr, gp_scr, a_scr, u_scr,
                  ycat_scr, hst_scr, *io_scr):
    step = pl.program_id(0)
    slot = lax.rem(step, 2)
    io_scr = list(io_scr)

    def relayout_copies(hbm_ref, vmem_ref, sem_ref, step_, slot_, to_vmem):
        t0 = pl.multiple_of(step_ * T_STEP, T_STEP)
        copies = []
        for b in range(BATCH):
            hbm = hbm_ref.at[b, pl.ds(t0, T_STEP), :]
            vmem = vmem_ref.at[slot_, :, b, :]
            src, dst = (hbm, vmem) if to_vmem else (vmem, hbm)
            copies.append(pltpu.make_async_copy(src, dst, sem_ref.at[slot_]))
        return copies

    if natural_in:
        xin_scr, in_sem = io_scr[:2]
        io_scr = io_scr[2:]
        in_copies = functools.partial(relayout_copies, x_ref, xin_scr, in_sem, to_vmem=True)

        @pl.when(step == 0)
        def _():
            for cp in in_copies(step, slot):
                cp.start()

        for cp in in_copies(step, slot):
            cp.wait()

        @pl.when(step + 1 < n_steps)
        def _():
            for cp in in_copies(step + 1, 1 - slot):
                cp.start()

        x_step = xin_scr.at[slot]
    else:
        x_step = x_ref

    if natural_out:
        xout_scr, out_sem = io_scr[:2]
        out_copies = functools.partial(relayout_copies, o_ref, xout_scr, out_sem, to_vmem=False)

        @pl.when(step >= 2)
        def _():
            for cp in out_copies(step - 2, slot):
                cp.wait()

        o_step = xout_scr.at[slot]
    else:
        o_step = o_ref

    @pl.when(step == 0)
    def _():
        xr_scr[T_CHUNK:T_CHUNK + CONV_HIST] = jnp.zeros((CONV_HIST, BATCH, D_RNN), F32)
        xp_scr[T_CHUNK:T_CHUNK + POOL_HIST] = jnp.zeros((POOL_HIST, BATCH, D_POOL), F32)
        hst_scr[...] = jnp.zeros((BATCH, D_RNN), F32)

    shift = mod_ref[0]
    gmod = vec_ref[V_PRE_G] * (1.0 + mod_ref[1])
    gate_g = mod_ref[2] * vec_ref[V_POST_G]
    lam = vec_ref[V_LAMBDA]
    softplus_neg_lam = jnp.maximum(-lam, 0.0) + jnp.log(1.0 + jnp.exp(-jnp.abs(lam)))
    half_log_a_scale = (-0.5 * LRU_C) * softplus_neg_lam

    for chunk in range(CHUNKS_PER_STEP):
        x_cur = x_step.at[pl.ds(chunk * T_CHUNK, T_CHUNK)]
        o_cur = o_step.at[pl.ds(chunk * T_CHUNK, T_CHUNK)]
        t_base = (step * CHUNKS_PER_STEP + chunk) * T_CHUNK

        xr_scr[0:CONV_HIST] = xr_scr[T_CHUNK:T_CHUNK + CONV_HIST]
        xp_scr[0:POOL_HIST] = xp_scr[T_CHUNK:T_CHUNK + POOL_HIST]

        for j in range(N_SUB):
            t0 = j * T_SUB
            xs = x_cur[pl.ds(t0, T_SUB)]
            ms = jnp.mean(xs * xs, axis=-1, keepdims=True)
            hh = xs * lax.rsqrt(ms + NORM_EPS) * gmod + shift
            hb_scr[_rows(t0, T_SUB), :] = hh.reshape(T_SUB * BATCH, D_MODEL).astype(BF16)

        hb = hb_scr[...]

        def proj(col0, width):
            return jnp.dot(hb, w_in_ref[:, col0:col0 + width],
                           preferred_element_type=F32).reshape(T_CHUNK, BATCH, width)

        xr_scr[CONV_HIST:CONV_HIST + T_CHUNK] = proj(0, D_RNN)
        for j in range(N_SUB):
            t0 = j * T_SUB
            acc = vec_ref[V_CONV_B] + vec_ref[V_CONV_W] * xr_scr[pl.ds(t0, T_SUB)]
            for k in range(1, CONV_WIDTH):
                acc = acc + vec_ref[V_CONV_W + k] * xr_scr[pl.ds(t0 + k, T_SUB)]
            u_scr[pl.ds(t0, T_SUB)] = acc
            ucb_scr[_rows(t0, T_SUB), :] = acc.reshape(T_SUB * BATCH, D_RNN).astype(BF16)

        xp_scr[POOL_HIST:POOL_HIST + T_CHUNK] = proj(2 * D_RNN, D_POOL)
        for p in range(N_RNN_HEADS // 2):
            pair = slice(2 * p * RNN_HEAD_DIM, (2 * p + 2) * RNN_HEAD_DIM)
            pre2 = jnp.dot(ucb_scr[:, pair], wg_ref[p], preferred_element_type=F32)
            pre2 = pre2.reshape(T_CHUNK, BATCH, 4 * RNN_HEAD_DIM) + bg_ref[
                :, p * 4 * RNN_HEAD_DIM:(p + 1) * 4 * RNN_HEAD_DIM]
            for q in range(2):
                h = 2 * p + q
                sl = slice(h * RNN_HEAD_DIM, (h + 1) * RNN_HEAD_DIM)
                c0 = 2 * q * RNN_HEAD_DIM
                tanh_r = jnp.tanh(pre2[:, :, c0:c0 + RNN_HEAD_DIM])
                tanh_i = jnp.tanh(pre2[:, :, c0 + RNN_HEAD_DIM:c0 + 2 * RNN_HEAD_DIM])
                hls = half_log_a_scale[:, sl]
                log_a = tanh_r * hls + hls
                a = jnp.exp(log_a)
                a_scr[:, :, sl] = a
                one_minus_a2 = jnp.tanh(log_a) * (-1.0 - a * a)
                mult = jnp.where(one_minus_a2 > 0.0, one_minus_a2 * lax.rsqrt(one_minus_a2), 0.0)
                u_scr[:, :, sl] = mult * ((tanh_i + 1.0) * u_scr[:, :, sl])

        gp_scr[...] = proj(2 * D_RNN + D_POOL, D_POOL)
        for g, win in enumerate(POOL_WINDOWS):
            sl = slice(g * POOL_GROUP_DIM, (g + 1) * POOL_GROUP_DIM)
            for j in range(N_SUB):
                t0 = j * T_SUB
                xv = xp_scr[pl.ds(t0 + POOL_HIST - (win - 1), T_SUB + win - 1), :, sl]
                s = xv
                d = 1
                while d < win:
                    s = s[d:] + s[:-d]
                    d *= 2
                if chunk == 0 and t0 < win:
                    t_glob = t_base + t0 + lax.broadcasted_iota(jnp.int32, (T_SUB, 1, 1), 0)
                    inv_count = 1.0 / jnp.minimum(t_glob + 1, win).astype(F32)
                else:
                    inv_count = 1.0 / win
                pooled = s * inv_count - xv[win - 1:]
                plb_scr[_rows(t0, T_SUB), sl] = pooled.reshape(
                    T_SUB * BATCH, POOL_GROUP_DIM).astype(BF16)

            yp = jnp.dot(plb_scr[:, sl], pool_w_ref[g], preferred_element_type=F32)
            yp = yp.reshape(T_CHUNK, BATCH, POOL_GROUP_DIM)
            yp = (yp + vec_ref[V_POOL_B, :, sl]) * vec_ref[V_POOL_S, :, sl]
            hg = gp_scr[:, :, sl]
            yp = yp * (hg * (jnp.tanh(hg) + 1.0))
            ycat_scr[:, D_RNN + g * POOL_GROUP_DIM:D_RNN + (g + 1) * POOL_GROUP_DIM] = (
                yp.reshape(ROWS, POOL_GROUP_DIM).astype(BF16))

        gr_scr[...] = proj(D_RNN, D_RNN)
        h_state = hst_scr[...]
        for t in range(T_CHUNK):
            h_state = a_scr[t] * h_state + u_scr[t]
            u_scr[t] = h_state
        hst_scr[...] = h_state

        for j in range(N_SUB):
            t0 = j * T_SUB
            hg = gr_scr[pl.ds(t0, T_SUB)]
            y = u_scr[pl.ds(t0, T_SUB)] * (hg * (jnp.tanh(hg) + 1.0))
            ycat_scr[_rows(t0, T_SUB), 0:D_RNN] = y.reshape(T_SUB * BATCH, D_RNN).astype(BF16)

        y_out = jnp.dot(ycat_scr[:, D_RNN:], w_out_ref[D_RNN:, :], preferred_element_type=F32)
        y_out = jnp.dot(ycat_scr[:, :D_RNN], w_out_ref[:D_RNN, :],
                        preferred_element_type=F32) + y_out
        a_scr[...] = y_out.reshape(T_CHUNK, BATCH, D_MODEL)
        for j in range(N_SUB):
            t0 = j * T_SUB
            y = a_scr[pl.ds(t0, T_SUB)]
            ms = jnp.mean(y * y, axis=-1, keepdims=True)
            o_cur[pl.ds(t0, T_SUB)] = (x_cur[pl.ds(t0, T_SUB)]
                                       + y * lax.rsqrt(ms + NORM_EPS) * gate_g)

    if natural_out:
        for cp in out_copies(step, slot):
            cp.start()

        @pl.when(step == n_steps - 1)
        def _():
            if n_steps >= 2:
                for cp in out_copies(step - 1, 1 - slot):
                    cp.wait()
            for cp in out_copies(step, slot):
                cp.wait()


def _layer(x_in, layer, natural_in, natural_out, mod, vecs, w_in, wg, bg, pool_w, w_out):
    seq = x_in.shape[1] if natural_in else x_in.shape[0]
    n_steps = seq // T_STEP
    assert seq % T_STEP == 0, (seq, T_STEP)

    def param_spec(arr):
        block = (None,) + arr.shape[1:]
        zeros = (0,) * (arr.ndim - 1)
        return pl.BlockSpec(block, lambda i: (layer,) + zeros, pipeline_mode=pl.Buffered(1))

    act_spec = pl.BlockSpec((T_STEP, BATCH, D_MODEL), lambda i: (i, 0, 0))
    hbm_spec = pl.BlockSpec(memory_space=pl.ANY)
    io_slots = [pltpu.VMEM((2, T_STEP, BATCH, D_MODEL), F32), pltpu.SemaphoreType.DMA((2,))]
    out_shape = (BATCH, seq, D_MODEL) if natural_out else (seq, BATCH, D_MODEL)
    params = (mod, vecs, w_in, wg, bg, pool_w, w_out)
    return pl.pallas_call(
        functools.partial(_layer_kernel, natural_in, natural_out, n_steps),
        grid=(n_steps,),
        in_specs=[hbm_spec if natural_in else act_spec] + [param_spec(p) for p in params],
        out_specs=hbm_spec if natural_out else act_spec,
        out_shape=jax.ShapeDtypeStruct(out_shape, F32),
        scratch_shapes=[
            pltpu.VMEM((ROWS, D_MODEL), BF16),
            pltpu.VMEM((ROWS, D_RNN), BF16),
            pltpu.VMEM((ROWS, D_POOL), BF16),
            pltpu.VMEM((CONV_HIST + T_CHUNK, BATCH, D_RNN), F32),
            pltpu.VMEM((T_CHUNK, BATCH, D_RNN), F32),
            pltpu.VMEM((POOL_HIST + T_CHUNK, BATCH, D_POOL), F32),
            pltpu.VMEM((T_CHUNK, BATCH, D_POOL), F32),
            pltpu.VMEM((T_CHUNK, BATCH, D_RNN), F32),
            pltpu.VMEM((T_CHUNK, BATCH, D_RNN), F32),
            pltpu.VMEM((ROWS, D_MIX), BF16),
            pltpu.VMEM((BATCH, D_RNN), F32),
        ] + (io_slots if natural_in else []) + (io_slots if natural_out else []),
        compiler_params=pltpu.CompilerParams(
            dimension_semantics=("arbitrary",),
            vmem_limit_bytes=VMEM_LIMIT_BYTES,
        ),
        name="hybrid_layer",
    )(x_in, *params)


def _sublane_bcast(v):
    return jnp.broadcast_to(v[..., None, :], v.shape[:-1] + (BATCH, v.shape[-1]))


def kernel(x, c, ada_w, ada_b, pre_norm_g, w_in, conv_w, conv_b, gate_a_w, gate_a_b, gate_x_w,
           gate_x_b, lru_lambda, pool_w, pool_b, pool_scale, w_out, post_norm_g):
    depth = ada_w.shape[0]
    mod = _modulation(c, ada_w, ada_b)

    col_scale = jnp.concatenate([jnp.ones((D_RNN,), F32), jnp.full((D_RNN,), 0.5, F32),
                                 jnp.ones((D_POOL,), F32), jnp.full((D_POOL,), 0.5, F32)])
    w_in_b = (w_in * col_scale).astype(BF16)
    w_out_b = w_out.astype(BF16)
    pool_w_b = pool_w.astype(BF16)
    wg = jnp.concatenate([gate_a_w, gate_x_w], axis=-1).astype(BF16)
    zero = jnp.zeros_like(wg[:, 0::2])
    wg = jnp.concatenate([jnp.concatenate([wg[:, 0::2], zero], axis=-1),
                          jnp.concatenate([zero, wg[:, 1::2]], axis=-1)],
                         axis=-2)
    bg = 0.5 * jnp.concatenate([gate_a_b, gate_x_b], axis=-1)
    bg = _sublane_bcast(bg.reshape(depth, N_RNN_HEADS * 2 * RNN_HEAD_DIM))
    vecs = jnp.concatenate([
        jnp.stack([pre_norm_g, post_norm_g, 0.5 * conv_b, lru_lambda,
                   pool_b.reshape(depth, D_POOL), pool_scale], axis=1),
        0.5 * conv_w], axis=1)
    vecs = _sublane_bcast(vecs)

    h = x
    for layer in range(depth):
        h = _layer(h, layer, layer == 0, layer == depth - 1,
                   mod, vecs, w_in_b, wg, bg, pool_w_b, w_out_b)
    return h
```

```python
import functools

import jax
import jax.numpy as jnp
from jax import lax
from jax.experimental import pallas as pl
from jax.experimental.pallas import tpu as pltpu

D_MODEL = 1024
BATCH = 8
D_RNN = D_MODEL
D_POOL = D_MODEL
D_MIX = D_RNN + D_POOL
N_RNN_HEADS = 8
RNN_HEAD_DIM = D_RNN // N_RNN_HEADS
CONV_WIDTH = 4
LRU_C = 8.0
POOL_WINDOWS = (2, 4, 8, 16)
POOL_GROUP_DIM = D_POOL // len(POOL_WINDOWS)
NORM_EPS = 1e-6

T_CHUNK = 64
ROWS = T_CHUNK * BATCH
CONV_HIST = CONV_WIDTH - 1
POOL_HIST = max(POOL_WINDOWS)
T_SUB = 8
N_SUB = T_CHUNK // T_SUB
VMEM_LIMIT_BYTES = 56 * 1024 * 1024

V_PRE_G, V_POST_G, V_CONV_B, V_LAMBDA, V_POOL_B, V_POOL_S, V_CONV_W = 0, 1, 2, 3, 4, 5, 6
N_VEC = V_CONV_W + CONV_WIDTH

F32 = jnp.float32
BF16 = jnp.bfloat16


def _mod_kernel(c_ref, w_ref, b_ref, o_ref):
    c = c_ref[...]
    c_act = c * (0.5 * jnp.tanh(0.5 * c) + 0.5)
    o_ref[0, 0] = jnp.dot(c_act, w_ref[0], preferred_element_type=F32) + b_ref[0, 0]


def _modulation(c, ada_w, ada_b):
    depth = ada_w.shape[0]
    return pl.pallas_call(
        _mod_kernel,
        grid=(depth, 3),
        in_specs=[
            pl.BlockSpec((BATCH, D_MODEL), lambda l, j: (0, 0)),
            pl.BlockSpec((1, D_MODEL, D_MODEL), lambda l, j: (l, 0, j)),
            pl.BlockSpec((1, 1, 1, D_MODEL), lambda l, j: (l, j, 0, 0)),
        ],
        out_specs=pl.BlockSpec((1, 1, BATCH, D_MODEL), lambda l, j: (l, j, 0, 0)),
        out_shape=jax.ShapeDtypeStruct((depth, 3, BATCH, D_MODEL), F32),
        name="ada_modulation",
    )(c, ada_w, ada_b.reshape(depth, 3, 1, D_MODEL))


def _rows(t0, n_t):
    return pl.ds(t0 * BATCH, n_t * BATCH)


def _layer_kernel(natural_in, natural_out, n_chunks,
                  x_hbm, mod_ref, vec_ref, w_in_ref, wg_ref, bg_ref, pool_w_ref, w_out_ref,
                  o_hbm,
                  xin_scr, ostage_scr, in_sem, out_sem,
                  hb_scr, ucb_scr, plb_scr, xr_scr, gr_scr, xp_scr, gp_scr, a_scr, u_scr,
                  ycat_scr, hst_scr):
    def io_copies(hbm_ref, vmem_ref, sem_ref, chunk, natural, to_vmem):
        if isinstance(chunk, int):
            t0, slot = chunk * T_CHUNK, chunk % 2
        else:
            t0, slot = pl.multiple_of(chunk * T_CHUNK, T_CHUNK), lax.rem(chunk, 2)
        if natural:
            pairs = [(hbm_ref.at[b, pl.ds(t0, T_CHUNK), :], vmem_ref.at[slot, :, b, :])
                     for b in range(BATCH)]
        else:
            pairs = [(hbm_ref.at[pl.ds(t0, T_CHUNK)], vmem_ref.at[slot])]
        return [pltpu.make_async_copy(*(p if to_vmem else p[::-1]), sem_ref.at[slot])
                for p in pairs]

    in_copies = functools.partial(io_copies, x_hbm, xin_scr, in_sem,
                                  natural=natural_in, to_vmem=True)
    out_copies = functools.partial(io_copies, o_hbm, ostage_scr, out_sem,
                                   natural=natural_out, to_vmem=False)

    xr_scr[T_CHUNK:T_CHUNK + CONV_HIST] = jnp.zeros((CONV_HIST, BATCH, D_RNN), F32)
    xp_scr[T_CHUNK:T_CHUNK + POOL_HIST] = jnp.zeros((POOL_HIST, BATCH, D_POOL), F32)
    hst_scr[...] = jnp.zeros((BATCH, D_RNN), F32)

    for cp in in_copies(0):
        cp.start()

    def chunk_body(chunk, carry):
        for cp in in_copies(chunk):
            cp.wait()

        @pl.when(chunk + 1 < n_chunks)
        def _():
            for cp in in_copies(chunk + 1):
                cp.start()

        @pl.when(chunk >= 2)
        def _():
            for cp in out_copies(chunk - 2):
                cp.wait()

        slot = lax.rem(chunk, 2)
        x_cur = xin_scr.at[slot]
        o_cur = ostage_scr.at[slot]
        t_base = chunk * T_CHUNK
        shift = mod_ref[0]
        gmod = vec_ref[V_PRE_G] * (1.0 + mod_ref[1])
        gate_g = mod_ref[2] * vec_ref[V_POST_G]
        lam = vec_ref[V_LAMBDA]
        softplus_neg_lam = jnp.maximum(-lam, 0.0) + jnp.log(1.0 + jnp.exp(-jnp.abs(lam)))
        half_log_a_scale = (-0.5 * LRU_C) * softplus_neg_lam

        xr_scr[0:CONV_HIST] = xr_scr[T_CHUNK:T_CHUNK + CONV_HIST]
        xp_scr[0:POOL_HIST] = xp_scr[T_CHUNK:T_CHUNK + POOL_HIST]

        for j in range(N_SUB):
            t0 = j * T_SUB
            xs = x_cur[pl.ds(t0, T_SUB)]
            ms = jnp.mean(xs * xs, axis=-1, keepdims=True)
            hh = xs * lax.rsqrt(ms + NORM_EPS) * gmod + shift
            hb_scr[_rows(t0, T_SUB), :] = hh.reshape(T_SUB * BATCH, D_MODEL).astype(BF16)

        hb = hb_scr[...]

        def proj(col0, width):
            return jnp.dot(hb, w_in_ref[:, col0:col0 + width],
                           preferred_element_type=F32).reshape(T_CHUNK, BATCH, width)

        xr_scr[CONV_HIST:CONV_HIST + T_CHUNK] = proj(0, D_RNN)
        for j in range(N_SUB):
            t0 = j * T_SUB
            acc = vec_ref[V_CONV_B] + vec_ref[V_CONV_W] * xr_scr[pl.ds(t0, T_SUB)]
            for k in range(1, CONV_WIDTH):
                acc = acc + vec_ref[V_CONV_W + k] * xr_scr[pl.ds(t0 + k, T_SUB)]
            u_scr[pl.ds(t0, T_SUB)] = acc
            ucb_scr[_rows(t0, T_SUB), :] = acc.reshape(T_SUB * BATCH, D_RNN).astype(BF16)

        xp_scr[POOL_HIST:POOL_HIST + T_CHUNK] = proj(2 * D_RNN, D_POOL)
        for p in range(N_RNN_HEADS // 2):
            pair = slice(2 * p * RNN_HEAD_DIM, (2 * p + 2) * RNN_HEAD_DIM)
            pre2 = jnp.dot(ucb_scr[:, pair], wg_ref[p], preferred_element_type=F32)
            pre2 = pre2.reshape(T_CHUNK, BATCH, 4 * RNN_HEAD_DIM) + bg_ref[
                :, p * 4 * RNN_HEAD_DIM:(p + 1) * 4 * RNN_HEAD_DIM]
            for q in range(2):
                h = 2 * p + q
                sl = slice(h * RNN_HEAD_DIM, (h + 1) * RNN_HEAD_DIM)
                c0 = 2 * q * RNN_HEAD_DIM
                tanh_r = jnp.tanh(pre2[:, :, c0:c0 + RNN_HEAD_DIM])
                tanh_i = jnp.tanh(pre2[:, :, c0 + RNN_HEAD_DIM:c0 + 2 * RNN_HEAD_DIM])
                hls = half_log_a_scale[:, sl]
                log_a = tanh_r * hls + hls
                a = jnp.exp(log_a)
                a_scr[:, :, sl] = a
                one_minus_a2 = jnp.tanh(log_a) * (-1.0 - a * a)
                mult = jnp.where(one_minus_a2 > 0.0, one_minus_a2 * lax.rsqrt(one_minus_a2), 0.0)
                u_scr[:, :, sl] = mult * ((tanh_i + 1.0) * u_scr[:, :, sl])

        gp_scr[...] = proj(2 * D_RNN + D_POOL, D_POOL)
        for g, win in enumerate(POOL_WINDOWS):
            sl = slice(g * POOL_GROUP_DIM, (g + 1) * POOL_GROUP_DIM)
            for j in range(N_SUB):
                t0 = j * T_SUB
                xv = xp_scr[pl.ds(t0 + POOL_HIST - (win - 1), T_SUB + win - 1), :, sl]
                s = xv
                d = 1
                while d < win:
                    s = s[d:] + s[:-d]
                    d *= 2
                if t0 < win:
                    t_glob = t_base + t0 + lax.broadcasted_iota(jnp.int32, (T_SUB, 1, 1), 0)
                    inv_count = 1.0 / jnp.minimum(t_glob + 1, win).astype(F32)
                else:
                    inv_count = 1.0 / win
                pooled = s * inv_count - xv[win - 1:]
                plb_scr[_rows(t0, T_SUB), sl] = pooled.reshape(
                    T_SUB * BATCH, POOL_GROUP_DIM).astype(BF16)

            yp = jnp.dot(plb_scr[:, sl], pool_w_ref[g], preferred_element_type=F32)
            yp = yp.reshape(T_CHUNK, BATCH, POOL_GROUP_DIM)
            yp = (yp + vec_ref[V_POOL_B, :, sl]) * vec_ref[V_POOL_S, :, sl]
            hg = gp_scr[:, :, sl]
            yp = yp * (hg * (jnp.tanh(hg) + 1.0))
            ycat_scr[:, D_RNN + g * POOL_GROUP_DIM:D_RNN + (g + 1) * POOL_GROUP_DIM] = (
                yp.reshape(ROWS, POOL_GROUP_DIM).astype(BF16))

        gr_scr[...] = proj(D_RNN, D_RNN)
        h_state = hst_scr[...]
        for t in range(T_CHUNK):
            h_state = a_scr[t] * h_state + u_scr[t]
            u_scr[t] = h_state
        hst_scr[...] = h_state

        for j in range(N_SUB):
            t0 = j * T_SUB
            hg = gr_scr[pl.ds(t0, T_SUB)]
            y = u_scr[pl.ds(t0, T_SUB)] * (hg * (jnp.tanh(hg) + 1.0))
            ycat_scr[_rows(t0, T_SUB), 0:D_RNN] = y.reshape(T_SUB * BATCH, D_RNN).astype(BF16)

        y_out = jnp.dot(ycat_scr[:, D_RNN:], w_out_ref[D_RNN:, :], preferred_element_type=F32)
        y_out = jnp.dot(ycat_scr[:, :D_RNN], w_out_ref[:D_RNN, :],
                        preferred_element_type=F32) + y_out
        a_scr[...] = y_out.reshape(T_CHUNK, BATCH, D_MODEL)
        for j in range(N_SUB):
            t0 = j * T_SUB
            y = a_scr[pl.ds(t0, T_SUB)]
            ms = jnp.mean(y * y, axis=-1, keepdims=True)
            o_cur[pl.ds(t0, T_SUB)] = (x_cur[pl.ds(t0, T_SUB)]
                                       + y * lax.rsqrt(ms + NORM_EPS) * gate_g)

        for cp in out_copies(chunk):
            cp.start()
        return carry

    lax.fori_loop(0, n_chunks, chunk_body, 0)
    for chunk in range(max(n_chunks - 2, 0), n_chunks):
        for cp in out_copies(chunk):
            cp.wait()


def _layer(x_in, layer, natural_in, natural_out, mod, vecs, w_in, wg, bg, pool_w, w_out):
    seq = x_in.shape[1] if natural_in else x_in.shape[0]
    n_chunks = seq // T_CHUNK
    assert seq % T_CHUNK == 0, (seq, T_CHUNK)

    def param_spec(arr):
        block = (None,) + arr.shape[1:]
        zeros = (0,) * (arr.ndim - 1)
        return pl.BlockSpec(block, lambda i: (layer,) + zeros, pipeline_mode=pl.Buffered(1))

    hbm_spec = pl.BlockSpec(memory_space=pl.ANY)
    out_shape = (BATCH, seq, D_MODEL) if natural_out else (seq, BATCH, D_MODEL)
    params = (mod, vecs, w_in, wg, bg, pool_w, w_out)
    return pl.pallas_call(
        functools.partial(_layer_kernel, natural_in, natural_out, n_chunks),
        grid=(1,),
        in_specs=[hbm_spec] + [param_spec(p) for p in params],
        out_specs=hbm_spec,
        out_shape=jax.ShapeDtypeStruct(out_shape, F32),
        scratch_shapes=[
            pltpu.VMEM((2, T_CHUNK, BATCH, D_MODEL), F32),
            pltpu.VMEM((2, T_CHUNK, BATCH, D_MODEL), F32),
            pltpu.SemaphoreType.DMA((2,)),
            pltpu.SemaphoreType.DMA((2,)),
            pltpu.VMEM((ROWS, D_MODEL), BF16),
            pltpu.VMEM((ROWS, D_RNN), BF16),
            pltpu.VMEM((ROWS, D_POOL), BF16),
            pltpu.VMEM((CONV_HIST + T_CHUNK, BATCH, D_RNN), F32),
            pltpu.VMEM((T_CHUNK, BATCH, D_RNN), F32),
            pltpu.VMEM((POOL_HIST + T_CHUNK, BATCH, D_POOL), F32),
            pltpu.VMEM((T_CHUNK, BATCH, D_POOL), F32),
            pltpu.VMEM((T_CHUNK, BATCH, D_RNN), F32),
            pltpu.VMEM((T_CHUNK, BATCH, D_RNN), F32),
            pltpu.VMEM((ROWS, D_MIX), BF16),
            pltpu.VMEM((BATCH, D_RNN), F32),
        ],
        compiler_params=pltpu.CompilerParams(
            dimension_semantics=("arbitrary",),
            vmem_limit_bytes=VMEM_LIMIT_BYTES,
        ),
        name="hybrid_layer",
    )(x_in, *params)


def _sublane_bcast(v):
    return jnp.broadcast_to(v[..., None, :], v.shape[:-1] + (BATCH, v.shape[-1]))


def kernel(x, c, ada_w, ada_b, pre_norm_g, w_in, conv_w, conv_b, gate_a_w, gate_a_b, gate_x_w,
           gate_x_b, lru_lambda, pool_w, pool_b, pool_scale, w_out, post_norm_g):
    depth = ada_w.shape[0]
    mod = _modulation(c, ada_w, ada_b)

    col_scale = jnp.concatenate([jnp.ones((D_RNN,), F32), jnp.full((D_RNN,), 0.5, F32),
                                 jnp.ones((D_POOL,), F32), jnp.full((D_POOL,), 0.5, F32)])
    w_in_b = (w_in * col_scale).astype(BF16)
    w_out_b = w_out.astype(BF16)
    pool_w_b = pool_w.astype(BF16)
    wg = jnp.concatenate([gate_a_w, gate_x_w], axis=-1).astype(BF16)
    zero = jnp.zeros_like(wg[:, 0::2])
    wg = jnp.concatenate([jnp.concatenate([wg[:, 0::2], zero], axis=-1),
                          jnp.concatenate([zero, wg[:, 1::2]], axis=-1)],
                         axis=-2)
    bg = 0.5 * jnp.concatenate([gate_a_b, gate_x_b], axis=-1)
    bg = _sublane_bcast(bg.reshape(depth, N_RNN_HEADS * 2 * RNN_HEAD_DIM))
    vecs = jnp.concatenate([
        jnp.stack([pre_norm_g, post_norm_g, 0.5 * conv_b, lru_lambda,
                   pool_b.reshape(depth, D_POOL), pool_scale], axis=1),
        0.5 * conv_w], axis=1)
    vecs = _sublane_bcast(vecs)

    h = x
    for layer in range(depth):
        h = _layer(h, layer, layer == 0, layer == depth - 1,
                   mod, vecs, w_in_b, wg, bg, pool_w_b, w_out_b)
    return h
```

```python
import functools

import jax
import jax.numpy as jnp
from jax import lax
from jax.experimental import pallas as pl
from jax.experimental.pallas import tpu as pltpu

D_MODEL = 1024
BATCH = 8
D_RNN = D_MODEL
D_POOL = D_MODEL
D_MIX = D_RNN + D_POOL
N_RNN_HEADS = 8
RNN_HEAD_DIM = D_RNN // N_RNN_HEADS
CONV_WIDTH = 4
LRU_C = 8.0
POOL_WINDOWS = (2, 4, 8, 16)
POOL_GROUP_DIM = D_POOL // len(POOL_WINDOWS)
NORM_EPS = 1e-6

T_CHUNK = 64
CHUNKS_PER_STEP = 2
T_STEP = T_CHUNK * CHUNKS_PER_STEP
ROWS = T_CHUNK * BATCH
CONV_HIST = CONV_WIDTH - 1
POOL_HIST = max(POOL_WINDOWS)
T_SUB = 8
N_SUB = T_CHUNK // T_SUB
VMEM_LIMIT_BYTES = 56 * 1024 * 1024

V_PRE_G, V_POST_G, V_CONV_B, V_LAMBDA, V_POOL_B, V_POOL_S, V_CONV_W = 0, 1, 2, 3, 4, 5, 6
N_VEC = V_CONV_W + CONV_WIDTH

F32 = jnp.float32
BF16 = jnp.bfloat16


def _mod_kernel(c_ref, w_ref, b_ref, o_ref):
    c = c_ref[...]
    c_act = c * (0.5 * jnp.tanh(0.5 * c) + 0.5)
    o_ref[0, 0] = jnp.dot(c_act, w_ref[0], preferred_element_type=F32) + b_ref[0, 0]


def _modulation(c, ada_w, ada_b):
    depth = ada_w.shape[0]
    return pl.pallas_call(
        _mod_kernel,
        grid=(depth, 3),
        in_specs=[
            pl.BlockSpec((BATCH, D_MODEL), lambda l, j: (0, 0)),
            pl.BlockSpec((1, D_MODEL, D_MODEL), lambda l, j: (l, 0, j)),
            pl.BlockSpec((1, 1, 1, D_MODEL), lambda l, j: (l, j, 0, 0)),
        ],
        out_specs=pl.BlockSpec((1, 1, BATCH, D_MODEL), lambda l, j: (l, j, 0, 0)),
        out_shape=jax.ShapeDtypeStruct((depth, 3, BATCH, D_MODEL), F32),
        name="ada_modulation",
    )(c, ada_w, ada_b.reshape(depth, 3, 1, D_MODEL))


def _rows(t0, n_t):
    return pl.ds(t0 * BATCH, n_t * BATCH)


def _layer_kernel(natural_in, natural_out, n_steps,
                  x_ref, mod_ref, vec_ref, w_in_ref, wg_ref, bg_ref, pool_w_ref, w_out_ref,
                  o_ref,
                  hb_scr, ucb_scr, plb_scr, xr_scr, gr_scr, xp_scr, gp_scr, a_scr, u_scr,
                  ycat_scr, hst_scr, *io_scr):
    step = pl.program_id(0)
    slot = lax.rem(step, 2)
    io_scr = list(io_scr)

    def relayout_copies(hbm_ref, vmem_ref, sem_ref, step_, slot_, to_vmem):
        t0 = pl.multiple_of(step_ * T_STEP, T_STEP)
        copies = []
        for b in range(BATCH):
            hbm = hbm_ref.at[b, pl.ds(t0, T_STEP), :]
            vmem = vmem_ref.at[slot_, :, b, :]
            src, dst = (hbm, vmem) if to_vmem else (vmem, hbm)
            copies.append(pltpu.make_async_copy(src, dst, sem_ref.at[slot_]))
        return copies

    if natural_in:
        xin_scr, in_sem = io_scr[:2]
        io_scr = io_scr[2:]
        in_copies = functools.partial(relayout_copies, x_ref, xin_scr, in_sem, to_vmem=True)

        @pl.when(step == 0)
        def _():
            for cp in in_copies(step, slot):
                cp.start()

        for cp in in_copies(step, slot):
            cp.wait()

        @pl.when(step + 1 < n_steps)
        def _():
            for cp in in_copies(step + 1, 1 - slot):
                cp.start()

        x_step = xin_scr.at[slot]
    else:
        x_step = x_ref

    if natural_out:
        xout_scr, out_sem = io_scr[:2]
        out_copies = functools.partial(relayout_copies, o_ref, xout_scr, out_sem, to_vmem=False)

        @pl.when(step >= 2)
        def _():
            for cp in out_copies(step - 2, slot):
                cp.wait()

        o_step = xout_scr.at[slot]
    else:
        o_step = o_ref

    @pl.when(step == 0)
    def _():
        xr_scr[T_CHUNK:T_CHUNK + CONV_HIST] = jnp.zeros((CONV_HIST, BATCH, D_RNN), F32)
        xp_scr[T_CHUNK:T_CHUNK + POOL_HIST] = jnp.zeros((POOL_HIST, BATCH, D_POOL), F32)
        hst_scr[...] = jnp.zeros((BATCH, D_RNN), F32)

    shift = mod_ref[0]
    gmod = vec_ref[V_PRE_G] * (1.0 + mod_ref[1])
    gate_g = mod_ref[2] * vec_ref[V_POST_G]
    lam = vec_ref[V_LAMBDA]
    softplus_neg_lam = jnp.maximum(-lam, 0.0) + jnp.log(1.0 + jnp.exp(-jnp.abs(lam)))
    half_log_a_scale = (-0.5 * LRU_C) * softplus_neg_lam

    def chunk_head(chunk):
        x_cur = x_step.at[pl.ds(chunk * T_CHUNK, T_CHUNK)]

        xr_scr[0:CONV_HIST] = xr_scr[T_CHUNK:T_CHUNK + CONV_HIST]
        xp_scr[0:POOL_HIST] = xp_scr[T_CHUNK:T_CHUNK + POOL_HIST]

        for j in range(N_SUB):
            t0 = j * T_SUB
            xs = x_cur[pl.ds(t0, T_SUB)]
            ms = jnp.mean(xs * xs, axis=-1, keepdims=True)
            hh = xs * lax.rsqrt(ms + NORM_EPS) * gmod + shift
            hb_scr[_rows(t0, T_SUB), :] = hh.reshape(T_SUB * BATCH, D_MODEL).astype(BF16)

    def proj(col0, width):
        return jnp.dot(hb_scr[...], w_in_ref[:, col0:col0 + width],
                       preferred_element_type=F32).reshape(T_CHUNK, BATCH, width)

    def chunk_rnn_input():
        xr_scr[CONV_HIST:CONV_HIST + T_CHUNK] = proj(0, D_RNN)
        for j in range(N_SUB):
            t0 = j * T_SUB
            acc = vec_ref[V_CONV_B] + vec_ref[V_CONV_W] * xr_scr[pl.ds(t0, T_SUB)]
            for k in range(1, CONV_WIDTH):
                acc = acc + vec_ref[V_CONV_W + k] * xr_scr[pl.ds(t0 + k, T_SUB)]
            u_scr[pl.ds(t0, T_SUB)] = acc
            ucb_scr[_rows(t0, T_SUB), :] = acc.reshape(T_SUB * BATCH, D_RNN).astype(BF16)

    def chunk_middle(chunk):
        t_base = (step * CHUNKS_PER_STEP + chunk) * T_CHUNK
        xp_scr[POOL_HIST:POOL_HIST + T_CHUNK] = proj(2 * D_RNN, D_POOL)
        for p in range(N_RNN_HEADS // 2):
            pair = slice(2 * p * RNN_HEAD_DIM, (2 * p + 2) * RNN_HEAD_DIM)
            pre2 = jnp.dot(ucb_scr[:, pair], wg_ref[p], preferred_element_type=F32)
            pre2 = pre2.reshape(T_CHUNK, BATCH, 4 * RNN_HEAD_DIM) + bg_ref[
                :, p * 4 * RNN_HEAD_DIM:(p + 1) * 4 * RNN_HEAD_DIM]
            for q in range(2):
                h = 2 * p + q
                sl = slice(h * RNN_HEAD_DIM, (h + 1) * RNN_HEAD_DIM)
                c0 = 2 * q * RNN_HEAD_DIM
                tanh_r = jnp.tanh(pre2[:, :, c0:c0 + RNN_HEAD_DIM])
                tanh_i = jnp.tanh(pre2[:, :, c0 + RNN_HEAD_DIM:c0 + 2 * RNN_HEAD_DIM])
                hls = half_log_a_scale[:, sl]
                log_a = tanh_r * hls + hls
                a = jnp.exp(log_a)
                a_scr[:, :, sl] = a
                one_minus_a2 = jnp.tanh(log_a) * (-1.0 - a * a)
                mult = jnp.where(one_minus_a2 > 0.0, one_minus_a2 * lax.rsqrt(one_minus_a2), 0.0)
                u_scr[:, :, sl] = mult * ((tanh_i + 1.0) * u_scr[:, :, sl])

        gp_scr[...] = proj(2 * D_RNN + D_POOL, D_POOL)
        for g, win in enumerate(POOL_WINDOWS):
            sl = slice(g * POOL_GROUP_DIM, (g + 1) * POOL_GROUP_DIM)
            for j in range(N_SUB):
                t0 = j * T_SUB
                xv = xp_scr[pl.ds(t0 + POOL_HIST - (win - 1), T_SUB + win - 1), :, sl]
                s = xv
                d = 1
                while d < win:
                    s = s[d:] + s[:-d]
                    d *= 2
                if chunk == 0 and t0 < win:
                    t_glob = t_base + t0 + lax.broadcasted_iota(jnp.int32, (T_SUB, 1, 1), 0)
                    inv_count = 1.0 / jnp.minimum(t_glob + 1, win).astype(F32)
                else:
                    inv_count = 1.0 / win
                pooled = s * inv_count - xv[win - 1:]
                plb_scr[_rows(t0, T_SUB), sl] = pooled.reshape(
                    T_SUB * BATCH, POOL_GROUP_DIM).astype(BF16)

            yp = jnp.dot(plb_scr[:, sl], pool_w_ref[g], preferred_element_type=F32)
            yp = yp.reshape(T_CHUNK, BATCH, POOL_GROUP_DIM)
            yp = (yp + vec_ref[V_POOL_B, :, sl]) * vec_ref[V_POOL_S, :, sl]
            hg = gp_scr[:, :, sl]
            yp = yp * (hg * (jnp.tanh(hg) + 1.0))
            ycat_scr[:, D_RNN + g * POOL_GROUP_DIM:D_RNN + (g + 1) * POOL_GROUP_DIM] = (
                yp.reshape(ROWS, POOL_GROUP_DIM).astype(BF16))

        gr_scr[...] = proj(D_RNN, D_RNN)
        h_state = hst_scr[...]
        for t in range(T_CHUNK):
            h_state = a_scr[t] * h_state + u_scr[t]
            u_scr[t] = h_state
        hst_scr[...] = h_state

        for j in range(N_SUB):
            t0 = j * T_SUB
            hg = gr_scr[pl.ds(t0, T_SUB)]
            y = u_scr[pl.ds(t0, T_SUB)] * (hg * (jnp.tanh(hg) + 1.0))
            ycat_scr[_rows(t0, T_SUB), 0:D_RNN] = y.reshape(T_SUB * BATCH, D_RNN).astype(BF16)

        a_scr[...] = jnp.dot(ycat_scr[:, D_RNN:], w_out_ref[D_RNN:, :],
                             preferred_element_type=F32).reshape(T_CHUNK, BATCH, D_MODEL)

    def chunk_tail(chunk):
        x_cur = x_step.at[pl.ds(chunk * T_CHUNK, T_CHUNK)]
        o_cur = o_step.at[pl.ds(chunk * T_CHUNK, T_CHUNK)]
        a_scr[...] = jnp.dot(ycat_scr[:, :D_RNN], w_out_ref[:D_RNN, :],
                             preferred_element_type=F32).reshape(T_CHUNK, BATCH, D_MODEL
                                                                 ) + a_scr[...]
        for j in range(N_SUB):
            t0 = j * T_SUB
            y = a_scr[pl.ds(t0, T_SUB)]
            ms = jnp.mean(y * y, axis=-1, keepdims=True)
            o_cur[pl.ds(t0, T_SUB)] = (x_cur[pl.ds(t0, T_SUB)]
                                       + y * lax.rsqrt(ms + NORM_EPS) * gate_g)

    chunk_head(0)
    chunk_rnn_input()
    for chunk in range(CHUNKS_PER_STEP):
        chunk_middle(chunk)
        if chunk + 1 < CHUNKS_PER_STEP:
            chunk_head(chunk + 1)
            chunk_rnn_input()
        chunk_tail(chunk)

    if natural_out:
        for cp in out_copies(step, slot):
            cp.start()

        @pl.when(step == n_steps - 1)
        def _():
            if n_steps >= 2:
                for cp in out_copies(step - 1, 1 - slot):
                    cp.wait()
            for cp in out_copies(step, slot):
                cp.wait()


def _layer(x_in, layer, natural_in, natural_out, mod, vecs, w_in, wg, bg, pool_w, w_out):
    seq = x_in.shape[1] if natural_in else x_in.shape[0]
    n_steps = seq // T_STEP
    assert seq % T_STEP == 0, (seq, T_STEP)

    def param_spec(arr):
        block = (None,) + arr.shape[1:]
        zeros = (0,) * (arr.ndim - 1)
        return pl.BlockSpec(block, lambda i: (layer,) + zeros, pipeline_mode=pl.Buffered(1))

    act_spec = pl.BlockSpec((T_STEP, BATCH, D_MODEL), lambda i: (i, 0, 0))
    hbm_spec = pl.BlockSpec(memory_space=pl.ANY)
    io_slots = [pltpu.VMEM((2, T_STEP, BATCH, D_MODEL), F32), pltpu.SemaphoreType.DMA((2,))]
    out_shape = (BATCH, seq, D_MODEL) if natural_out else (seq, BATCH, D_MODEL)
    params = (mod, vecs, w_in, wg, bg, pool_w, w_out)
    return pl.pallas_call(
        functools.partial(_layer_kernel, natural_in, natural_out, n_steps),
        grid=(n_steps,),
        in_specs=[hbm_spec if natural_in else act_spec] + [param_spec(p) for p in params],
        out_specs=hbm_spec if natural_out else act_spec,
        out_shape=jax.ShapeDtypeStruct(out_shape, F32),
        scratch_shapes=[
            pltpu.VMEM((ROWS, D_MODEL), BF16),
            pltpu.VMEM((ROWS, D_RNN), BF16),
            pltpu.VMEM((ROWS, D_POOL), BF16),
            pltpu.VMEM((CONV_HIST + T_CHUNK, BATCH, D_RNN), F32),
            pltpu.VMEM((T_CHUNK, BATCH, D_RNN), F32),
            pltpu.VMEM((POOL_HIST + T_CHUNK, BATCH, D_POOL), F32),
            pltpu.VMEM((T_CHUNK, BATCH, D_POOL), F32),
            pltpu.VMEM((T_CHUNK, BATCH, D_RNN), F32),
            pltpu.VMEM((T_CHUNK, BATCH, D_RNN), F32),
            pltpu.VMEM((ROWS, D_MIX), BF16),
            pltpu.VMEM((BATCH, D_RNN), F32),
        ] + (io_slots if natural_in else []) + (io_slots if natural_out else []),
        compiler_params=pltpu.CompilerParams(
            dimension_semantics=("arbitrary",),
            vmem_limit_bytes=VMEM_LIMIT_BYTES,
        ),
        name="hybrid_layer",
    )(x_in, *params)


def _sublane_bcast(v):
    return jnp.broadcast_to(v[..., None, :], v.shape[:-1] + (BATCH, v.shape[-1]))


def kernel(x, c, ada_w, ada_b, pre_norm_g, w_in, conv_w, conv_b, gate_a_w, gate_a_b, gate_x_w,
           gate_x_b, lru_lambda, pool_w, pool_b, pool_scale, w_out, post_norm_g):
    depth = ada_w.shape[0]
    mod = _modulation(c, ada_w, ada_b)

    col_scale = jnp.concatenate([jnp.ones((D_RNN,), F32), jnp.full((D_RNN,), 0.5, F32),
                                 jnp.ones((D_POOL,), F32), jnp.full((D_POOL,), 0.5, F32)])
    w_in_b = (w_in * col_scale).astype(BF16)
    w_out_b = w_out.astype(BF16)
    pool_w_b = pool_w.astype(BF16)
    wg = jnp.concatenate([gate_a_w, gate_x_w], axis=-1).astype(BF16)
    zero = jnp.zeros_like(wg[:, 0::2])
    wg = jnp.concatenate([jnp.concatenate([wg[:, 0::2], zero], axis=-1),
                          jnp.concatenate([zero, wg[:, 1::2]], axis=-1)],
                         axis=-2)
    bg = 0.5 * jnp.concatenate([gate_a_b, gate_x_b], axis=-1)
    bg = _sublane_bcast(bg.reshape(depth, N_RNN_HEADS * 2 * RNN_HEAD_DIM))
    vecs = jnp.concatenate([
        jnp.stack([pre_norm_g, post_norm_g, 0.5 * conv_b, lru_lambda,
                   pool_b.reshape(depth, D_POOL), pool_scale], axis=1),
        0.5 * conv_w], axis=1)
    vecs = _sublane_bcast(vecs)

    h = x
    for layer in range(depth):
        h = _layer(h, layer, layer == 0, layer == depth - 1,
                   mod, vecs, w_in_b, wg, bg, pool_w_b, w_out_b)
    return h
```
